```python
import math
import jax, jax.numpy as jnp
from jax import lax
import numpy as np

D_MODEL = 1024
BATCH = 16
SEQ = 2048
DEPTH = 4
DEC_BATCH = 16
DEC_SEQ = 4096
PAST_LEN = 128

N_MIXERS = 4
GRID_W = 64
EPS = 1e-6
ROPE_THETA = 10000.0
DA_HEAD_DIM = 64
DA_HEADS = D_MODEL // (2 * DA_HEAD_DIM)
Q_BLOCK = 128
S5_GROUP = 16
S5_GROUPS = D_MODEL // S5_GROUP
S5_STATE = 64
NA_HEAD_DIM = 64
NA_HEADS = D_MODEL // NA_HEAD_DIM
NA_WIN_ROWS = 8
NA_WIN_COLS = 16
DN_HEAD_DIM = 128
DN_HEADS = D_MODEL // DN_HEAD_DIM
DN_CONV = 4
DN_CHUNK = 64
FFN_HIDDEN = -(-8 * D_MODEL // (3 * 256)) * 256

kernel_name = 'hybrid_bidir_encoder'


def rms_norm(x, g):
    xf = x.astype(jnp.float32)
    y = xf * lax.rsqrt(jnp.mean(xf * xf, axis=-1, keepdims=True) + EPS)
    return (y * g.astype(jnp.float32)).astype(x.dtype)


def apply_rope(x):
    l, d = x.shape[1], x.shape[-1]
    half = d // 2
    inv = jnp.power(ROPE_THETA, -jnp.arange(half, dtype=jnp.float32) * 2.0 / d)
    ang = jnp.arange(l, dtype=jnp.float32)[:, None] * inv[None, :]
    shape = (1, l) + (1,) * (x.ndim - 3) + (half,)
    cos = jnp.cos(ang).reshape(shape)
    sin = jnp.sin(ang).reshape(shape)
    xf = x.astype(jnp.float32)
    x1, x2 = xf[..., :half], xf[..., half:]
    return jnp.concatenate([x1 * cos - x2 * sin, x2 * cos + x1 * sin], axis=-1).astype(x.dtype)


def swiglu(h, w1, w3, w2):
    return (jax.nn.silu(h @ w1) * (h @ w3)) @ w2


def diff_attention(h, w_in, lam, subln_g, w_out, lambda_init):
    b, l, _ = h.shape
    q, k, v = jnp.split(h @ w_in, 3, axis=-1)
    q = apply_rope(q.reshape(b, l, DA_HEADS, 2, DA_HEAD_DIM))
    k = apply_rope(k.reshape(b, l, DA_HEADS, 2, DA_HEAD_DIM))
    v = v.reshape(b, l, DA_HEADS, 2 * DA_HEAD_DIM)
    lam = lam.astype(jnp.float32)
    lam_full = jnp.exp(jnp.sum(lam[0] * lam[1])) - jnp.exp(jnp.sum(lam[2] * lam[3])) + lambda_init
    scale = DA_HEAD_DIM ** -0.5
    n_blk = l // Q_BLOCK
    q_blocks = q.reshape(b, n_blk, Q_BLOCK, DA_HEADS, 2, DA_HEAD_DIM).transpose(1, 0, 2, 3, 4, 5)

    def block(q_blk):
        s = jnp.einsum('bqhcd,bkhcd->bhcqk', q_blk, k).astype(jnp.float32) * scale
        p = jax.nn.softmax(s, axis=-1)
        a = p[:, :, 0] - lam_full * p[:, :, 1]
        return jnp.einsum('bhqk,bkhe->bqhe', a.astype(v.dtype), v)

    o = lax.map(block, q_blocks)
    o = o.transpose(1, 0, 2, 3, 4).reshape(b, l, DA_HEADS, 2 * DA_HEAD_DIM)
    o = rms_norm(o, subln_g) * (1.0 - lambda_init)
    return o.reshape(b, l, D_MODEL) @ w_out


def _recurrence_combine(left, right):
    a_l, b_l = left
    a_r, b_r = right
    return a_r * a_l, a_r * b_l + b_r


def s5_mixer(h, w_in, a_re, a_im, log_dt, b_re, b_im, c_re, c_im, d_skip, w_glu):
    b, l, _ = h.shape
    f32 = jnp.float32
    u = h @ w_in
    lam = lax.complex(a_re.astype(f32), a_im.astype(f32))
    dt = jnp.exp(log_dt.astype(f32))[..., None]
    lam_bar = jnp.exp(lam * dt)
    b_mat = lax.complex(b_re.astype(f32), b_im.astype(f32))
    b_bar = ((lam_bar - 1.0) / lam)[..., None] * b_mat
    c_mat = lax.complex(c_re.astype(f32), c_im.astype(f32))

    def one_sequence(us):
        usc = us.astype(jnp.complex64)

        def direction(dr, reverse):
            bu = jnp.einsum('lgp,gnp->lgn', usc, b_bar[dr])
            a = jnp.broadcast_to(lam_bar[dr], bu.shape)
            _, states = lax.associative_scan(_recurrence_combine, (a, bu), reverse=reverse, axis=0)
            return jnp.real(jnp.einsum('gpn,lgn->lgp', c_mat[dr], states))

        return direction(0, False) + direction(1, True)

    y = lax.map(one_sequence, u.astype(f32).reshape(b, l, S5_GROUPS, S5_GROUP))
    y = y.reshape(b, l, D_MODEL) + d_skip.astype(f32) * u.astype(f32)
    gl, gt = jnp.split(jax.nn.gelu(y).astype(h.dtype) @ w_glu, 2, axis=-1)
    return gl * jax.nn.sigmoid(gt)


def neighborhood_attention(h, w_in, rpb, w_out):
    b, l, _ = h.shape
    rows = l // GRID_W
    kr = min(NA_WIN_ROWS, rows)
    kc = NA_WIN_COLS
    q, k, v = [t.reshape(b, rows, GRID_W, NA_HEADS, NA_HEAD_DIM) for t in jnp.split(h @ w_in, 3, axis=-1)]
    row_ids = jnp.arange(rows)
    row_start = jnp.clip(row_ids - kr // 2, 0, rows - kr)
    cols = jnp.arange(GRID_W)
    col_start = jnp.clip(cols - kc // 2, 0, GRID_W - kc)
    col_valid = (cols[None, :] >= col_start[:, None]) & (cols[None, :] < col_start[:, None] + kc)
    col_bias_idx = jnp.clip(cols[None, :] - cols[:, None], -(kc - 1), kc - 1) + NA_WIN_COLS - 1
    scale = NA_HEAD_DIM ** -0.5

    def row_block(args):
        q_r, r, rs = args
        k_band = lax.dynamic_slice_in_dim(k, rs, kr, axis=1)
        v_band = lax.dynamic_slice_in_dim(v, rs, kr, axis=1)
        row_bias_idx = rs + jnp.arange(kr) - r + NA_WIN_ROWS - 1
        bias = rpb[:, row_bias_idx[None, :, None], col_bias_idx[:, None, :]]
        s = jnp.einsum('bqhd,bikhd->bhqik', q_r, k_band).astype(jnp.float32) * scale
        s = s + bias.astype(jnp.float32)[None]
        s = jnp.where(col_valid[:, None, :], s, -1e30)
        p = jax.nn.softmax(s.reshape(b, NA_HEADS, GRID_W, kr * GRID_W), axis=-1)
        p = p.reshape(b, NA_HEADS, GRID_W, kr, GRID_W)
        return jnp.einsum('bhqik,bikhd->bqhd', p.astype(v.dtype), v_band)

    o = lax.map(row_block, (q.transpose(1, 0, 2, 3, 4), row_ids, row_start))
    o = o.transpose(1, 0, 2, 3, 4).reshape(b, l, D_MODEL)
    return o @ w_out


def centred_depthwise_conv(x, w):
    kk, ch = w.shape
    return lax.conv_general_dilated(x, w[:, None, :], window_strides=(1,),
                                    padding=[((kk - 1) // 2, kk // 2)],
                                    dimension_numbers=('NWC', 'WIO', 'NWC'),
                                    feature_group_count=ch)


def l2_normalize(x):
    return x * lax.rsqrt(jnp.sum(x * x, axis=-1, keepdims=True) + EPS)


def chunk_gated_delta(q, k, v, g, beta):
    b, l, nh, dk = q.shape
    dv = v.shape[-1]
    n = l // DN_CHUNK
    c = DN_CHUNK

    def to_chunks(t):
        return t.reshape(b, n, c, nh, -1).transpose(0, 3, 1, 2, 4)

    q, k, v = to_chunks(q), to_chunks(k), to_chunks(v)
    g = g.reshape(b, n, c, nh).transpose(0, 3, 1, 2)
    beta = beta.reshape(b, n, c, nh).transpose(0, 3, 1, 2)
    gc = jnp.cumsum(g, axis=-1)
    causal = jnp.tril(jnp.ones((c, c), dtype=bool))
    strict = jnp.tril(jnp.ones((c, c), dtype=bool), k=-1)
    diff = gc[..., :, None] - gc[..., None, :]
    decay = jnp.where(causal, jnp.exp(jnp.where(causal, diff, 0.0)), 0.0)
    kb = k * beta[..., None]
    m = jnp.where(strict, jnp.einsum('bhncd,bhnsd->bhncs', kb, k) * decay, 0.0)
    rhs = jnp.concatenate([v * beta[..., None], kb * jnp.exp(gc)[..., None]], axis=-1)
    sol = lax.linalg.triangular_solve(jnp.eye(c, dtype=jnp.float32) + m, rhs,
                                      left_side=True, lower=True, unit_diagonal=True)
    u, w = sol[..., :dv], sol[..., dv:]
    attn = jnp.einsum('bhncd,bhnsd->bhncs', q, k) * decay
    q_dec = q * jnp.exp(gc)[..., None]
    k_dec = k * jnp.exp(gc[..., -1:] - gc)[..., None]
    chunk_decay = jnp.exp(gc[..., -1])

    def step(state, xs):
        u_i, w_i, qd_i, kd_i, attn_i, cd_i = xs
        v_new = u_i - jnp.einsum('bhcd,bhde->bhce', w_i, state)
        o = jnp.einsum('bhcd,bhde->bhce', qd_i, state) + jnp.einsum('bhcs,bhse->bhce', attn_i, v_new)
        state = state * cd_i[..., None, None] + jnp.einsum('bhcd,bhce->bhde', kd_i, v_new)
        return state, o

    xs = tuple(jnp.moveaxis(t, 2, 0) for t in (u, w, q_dec, k_dec, attn, chunk_decay))
    s0 = jnp.zeros((b, nh, dk, dv), jnp.float32)
    _, o = lax.scan(step, s0, xs)
    return o.transpose(1, 0, 3, 2, 4).reshape(b, l, nh, dv)


def gated_deltanet(h, w_in, conv_w, a_log, dt_bias, onorm_g, w_out):
    b, l, _ = h.shape
    f32 = jnp.float32
    proj = h @ w_in
    qkv = jax.nn.silu(centred_depthwise_conv(proj[..., :3 * D_MODEL], conv_w))
    z = proj[..., 3 * D_MODEL:4 * D_MODEL]
    ab = proj[..., 4 * D_MODEL:].reshape(b, l, 4, DN_HEADS).astype(f32)
    q, k, v = [t.reshape(b, l, DN_HEADS, DN_HEAD_DIM).astype(f32) for t in jnp.split(qkv, 3, axis=-1)]
    q = l2_normalize(q) * (DN_HEAD_DIM ** -0.5)
    k = l2_normalize(k)
    a_log = a_log.astype(f32)
    dt_bias = dt_bias.astype(f32)
    g_f = -jnp.exp(a_log[0]) * jax.nn.softplus(ab[:, :, 0] + dt_bias[0])
    g_b = -jnp.exp(a_log[1]) * jax.nn.softplus(ab[:, :, 1] + dt_bias[1])
    beta_f = jax.nn.sigmoid(ab[:, :, 2])
    beta_b = jax.nn.sigmoid(ab[:, :, 3])
    o_f = chunk_gated_delta(q, k, v, g_f, beta_f)
    flip = lambda t: jnp.flip(t, axis=1)
    o_b = flip(chunk_gated_delta(flip(q), flip(k), flip(v), flip(g_b), flip(beta_b)))
    o = rms_norm(o_f + o_b, onorm_g) * jax.nn.silu(z.reshape(b, l, DN_HEADS, DN_HEAD_DIM).astype(f32))
    return o.astype(h.dtype).reshape(b, l, D_MODEL) @ w_out


def encoder_trunk(x, c, p):
    for i in range(DEPTH):
        m, j = i % N_MIXERS, i // N_MIXERS
        mod = (jax.nn.silu(c) @ p['ada_w'][i] + p['ada_b'][i])[:, None, :]
        sh1, sc1, g1, sh2, sc2, g2 = jnp.split(mod, 6, axis=-1)
        hmix = rms_norm(x, p['norm1_g'][i]) * (1.0 + sc1) + sh1
        if m == 0:
            lambda_init = 0.8 - 0.6 * math.exp(-0.3 * i)
            y = diff_attention(hmix, p['da_w_in'][j], p['da_lam'][j], p['da_subln_g'][j],
                               p['da_w_out'][j], lambda_init)
        elif m == 1:
            y = s5_mixer(hmix, p['s5_w_in'][j], p['s5_a_re'][j], p['s5_a_im'][j], p['s5_log_dt'][j],
                         p['s5_b_re'][j], p['s5_b_im'][j], p['s5_c_re'][j], p['s5_c_im'][j],
                         p['s5_d'][j], p['s5_w_glu'][j])
        elif m == 2:
            y = neighborhood_attention(hmix, p['na_w_in'][j], p['na_rpb'][j], p['na_w_out'][j])
        else:
            y = gated_deltanet(hmix, p['dn_w_in'][j], p['dn_conv_w'][j], p['dn_a_log'][j],
                               p['dn_dt_bias'][j], p['dn_onorm_g'][j], p['dn_w_out'][j])
        x = x + g1 * y
        hffn = rms_norm(x, p['norm2_g'][i]) * (1.0 + sc2) + sh2
        x = x + g2 * swiglu(hffn, p['ffn_w1'][i], p['ffn_w3'][i], p['ffn_w2'][i])
    return rms_norm(x, p['final_g'])


def setup_inputs(seed: int = 0) -> dict:
    key = jax.random.key(seed)
    ks = iter(jax.random.split(key, 64))
    f32 = jnp.float32

    def nrm(shape, scale):
        return jax.random.normal(next(ks), shape, f32) * scale

    def unif(shape, lo, hi):
        return jax.random.uniform(next(ks), shape, f32, lo, hi)

    n_a, n_s, n_na, n_dn = [len(range(m, DEPTH, N_MIXERS)) for m in range(N_MIXERS)]
    d, f = D_MODEL, FFN_HIDDEN
    g, pp, ns = S5_GROUPS, S5_GROUP, S5_STATE
    x_prompt = nrm((BATCH, SEQ, d), 1.0)
    x_sample = nrm((DEC_BATCH, DEC_SEQ, d), 1.0)
    c_prompt = nrm((BATCH, d), 1.0)
    c_sample = nrm((DEC_BATCH, d), 1.0)
    ada_w = nrm((DEPTH, d, 6 * d), 0.5 * d ** -0.5)
    ada_b = nrm((DEPTH, 6 * d), 0.02)
    norm1_g = 1.0 + nrm((DEPTH, d), 0.02)
    norm2_g = 1.0 + nrm((DEPTH, d), 0.02)
    ffn_w1 = nrm((DEPTH, d, f), d ** -0.5)
    ffn_w3 = nrm((DEPTH, d, f), d ** -0.5)
    ffn_w2 = nrm((DEPTH, f, d), f ** -0.5)
    da_w_in = nrm((n_a, d, 3 * d), d ** -0.5)
    da_lam = nrm((n_a, 4, DA_HEAD_DIM), 0.1)
    da_subln_g = 1.0 + nrm((n_a, 2 * DA_HEAD_DIM), 0.02)
    da_w_out = nrm((n_a, d, d), d ** -0.5)
    s5_w_in = nrm((n_s, d, d), d ** -0.5)
    s5_a_re = -0.5 + nrm((n_s, 2, g, ns), 0.01)
    s5_a_im = math.pi * jnp.arange(ns, dtype=f32) + nrm((n_s, 2, g, ns), 0.01)
    s5_log_dt = unif((n_s, 2, g), math.log(1e-3), math.log(1e-1))
    s5_b_re = nrm((n_s, 2, g, ns, pp), (2 * pp) ** -0.5)
    s5_b_im = nrm((n_s, 2, g, ns, pp), (2 * pp) ** -0.5)
    s5_c_re = nrm((n_s, 2, g, pp, ns), (2 * ns) ** -0.5)
    s5_c_im = nrm((n_s, 2, g, pp, ns), (2 * ns) ** -0.5)
    s5_d = nrm((n_s, d), 1.0)
    s5_w_glu = nrm((n_s, d, 2 * d), d ** -0.5)
    na_w_in = nrm((n_na, d, 3 * d), d ** -0.5)
    na_rpb = nrm((n_na, NA_HEADS, 2 * NA_WIN_ROWS - 1, 2 * NA_WIN_COLS - 1), 0.1)
    na_w_out = nrm((n_na, d, d), d ** -0.5)
    dn_w_in = nrm((n_dn, d, 4 * d + 4 * DN_HEADS), d ** -0.5)
    dn_conv_w = nrm((n_dn, DN_CONV, 3 * d), DN_CONV ** -0.5)
    dn_a_log = jnp.log(unif((n_dn, 2, DN_HEADS), 1.0, 16.0))
    dt = jnp.exp(unif((n_dn, 2, DN_HEADS), math.log(1e-3), math.log(1e-1)))
    dn_dt_bias = dt + jnp.log(-jnp.expm1(-dt))
    dn_onorm_g = 1.0 + nrm((n_dn, DN_HEAD_DIM), 0.02)
    dn_w_out = nrm((n_dn, d, d), d ** -0.5)
    final_g = 1.0 + nrm((d,), 0.02)
    return {
        'x_prompt': x_prompt, 'x_sample': x_sample, 'c_prompt': c_prompt, 'c_sample': c_sample,
        'ada_w': ada_w, 'ada_b': ada_b, 'norm1_g': norm1_g, 'norm2_g': norm2_g,
        'ffn_w1': ffn_w1, 'ffn_w3': ffn_w3, 'ffn_w2': ffn_w2,
        'da_w_in': da_w_in, 'da_lam': da_lam, 'da_subln_g': da_subln_g, 'da_w_out': da_w_out,
        's5_w_in': s5_w_in, 's5_a_re': s5_a_re, 's5_a_im': s5_a_im, 's5_log_dt': s5_log_dt,
        's5_b_re': s5_b_re, 's5_b_im': s5_b_im, 's5_c_re': s5_c_re, 's5_c_im': s5_c_im,
        's5_d': s5_d, 's5_w_glu': s5_w_glu,
        'na_w_in': na_w_in, 'na_rpb': na_rpb, 'na_w_out': na_w_out,
        'dn_w_in': dn_w_in, 'dn_conv_w': dn_conv_w, 'dn_a_log': dn_a_log, 'dn_dt_bias': dn_dt_bias,
        'dn_onorm_g': dn_onorm_g, 'dn_w_out': dn_w_out,
        'final_g': final_g,
    }


def reference(x_prompt, x_sample, c_prompt, c_sample,
              ada_w, ada_b, norm1_g, norm2_g, ffn_w1, ffn_w3, ffn_w2,
              da_w_in, da_lam, da_subln_g, da_w_out,
              s5_w_in, s5_a_re, s5_a_im, s5_log_dt, s5_b_re, s5_b_im, s5_c_re, s5_c_im, s5_d, s5_w_glu,
              na_w_in, na_rpb, na_w_out,
              dn_w_in, dn_conv_w, dn_a_log, dn_dt_bias, dn_onorm_g, dn_w_out,
              final_g):
    params = dict(
        ada_w=ada_w, ada_b=ada_b, norm1_g=norm1_g, norm2_g=norm2_g,
        ffn_w1=ffn_w1, ffn_w3=ffn_w3, ffn_w2=ffn_w2,
        da_w_in=da_w_in, da_lam=da_lam, da_subln_g=da_subln_g, da_w_out=da_w_out,
        s5_w_in=s5_w_in, s5_a_re=s5_a_re, s5_a_im=s5_a_im, s5_log_dt=s5_log_dt,
        s5_b_re=s5_b_re, s5_b_im=s5_b_im, s5_c_re=s5_c_re, s5_c_im=s5_c_im,
        s5_d=s5_d, s5_w_glu=s5_w_glu,
        na_w_in=na_w_in, na_rpb=na_rpb, na_w_out=na_w_out,
        dn_w_in=dn_w_in, dn_conv_w=dn_conv_w, dn_a_log=dn_a_log, dn_dt_bias=dn_dt_bias,
        dn_onorm_g=dn_onorm_g, dn_w_out=dn_w_out,
        final_g=final_g,
    )
    y_prompt = encoder_trunk(x_prompt, c_prompt, params)
    y_sample = encoder_trunk(x_sample, c_sample, params)
    return (y_prompt, y_sample)
```

```python
import functools
import math

import jax
import jax.numpy as jnp
from jax import lax
from jax.experimental import pallas as pl
from jax.experimental.pallas import tpu as pltpu

F32 = jnp.float32
BF16 = jnp.bfloat16
HI = lax.Precision.HIGHEST

EPS = 1e-6
ROPE_THETA = 10000.0
N_MIXERS = 4
GRID_W = 64
DA_HEAD_DIM = 64
Q_BLOCK = 128
S5_GROUP = 16
S5_STATE = 64
S5_CHUNK = 16
NA_HEAD_DIM = 64
NA_WIN_ROWS = 8
NA_WIN_COLS = 16
DN_HEAD_DIM = 128
DN_CONV = 4
DN_CHUNK = 64

V7X_VMEM_LIMIT = 52 * 1024 * 1024
LANES = 128


def _cparams(sem):
    return pltpu.CompilerParams(dimension_semantics=sem, vmem_limit_bytes=V7X_VMEM_LIMIT)


def _rms(xf, g):
    return xf * lax.rsqrt(jnp.mean(xf * xf, axis=-1, keepdims=True) + EPS) * g


def _dot(a, b):
    return jnp.dot(a, b, preferred_element_type=F32)


def _dot_nt(a, b):
    return lax.dot_general(a, b, (((1,), (1,)), ((), ())), preferred_element_type=F32)


def _dot_tn(a, b):
    return lax.dot_general(a, b, (((0,), (0,)), ((), ())), preferred_element_type=F32)


def _adaln_kernel(c_ref, w_ref, b_ref, o_ref):
    c = c_ref[...]
    a = c * jax.nn.sigmoid(c)
    o_ref[0] = jnp.dot(a, w_ref[0], preferred_element_type=F32, precision=HI) + b_ref[0]


def adaln(c, ada_w, ada_b):
    depth, d, n = ada_w.shape
    rows = c.shape[0]
    tn = 1536 if n % 1536 == 0 else n
    return pl.pallas_call(
        _adaln_kernel,
        grid=(depth, n // tn),
        in_specs=[
            pl.BlockSpec((rows, d), lambda i, j: (0, 0)),
            pl.BlockSpec((1, d, tn), lambda i, j: (i, 0, j)),
            pl.BlockSpec((1, 1, tn), lambda i, j: (i, 0, j)),
        ],
        out_specs=pl.BlockSpec((1, rows, tn), lambda i, j: (i, 0, j)),
        out_shape=jax.ShapeDtypeStruct((depth, rows, n), F32),
        compiler_params=_cparams(("parallel", "parallel")),
        name="adaln",
    )(c, ada_w, ada_b.reshape(depth, 1, n))


def _swap_halves(a, half):
    n = a.shape[-1]
    lane = lax.broadcasted_iota(jnp.int32, a.shape, a.ndim - 1)
    first = (lane % (2 * half)) < half
    return jnp.where(first, pltpu.roll(a, n - half, a.ndim - 1), pltpu.roll(a, half, a.ndim - 1))


def _nm_matmul_kernel(x_ref, g_ref, sc_ref, sh_ref, w_ref, *rest, rope_tiles):
    if rope_tiles:
        cos_ref, sin_ref, o_ref, h_ref = rest
    else:
        o_ref, h_ref = rest
    j = pl.program_id(2)

    @pl.when(j == 0)
    def _():
        h = _rms(x_ref[0], g_ref[...]) * (1.0 + sc_ref[0]) + sh_ref[0]
        h_ref[...] = h.astype(BF16)

    acc = _dot(h_ref[...], w_ref[...])
    if rope_tiles:
        @pl.when(j < rope_tiles)
        def _():
            r = acc * cos_ref[...] + _swap_halves(acc, DA_HEAD_DIM // 2) * sin_ref[...]
            o_ref[0] = r.astype(o_ref.dtype)

        @pl.when(j >= rope_tiles)
        def _():
            o_ref[0] = acc.astype(o_ref.dtype)
    else:
        o_ref[0] = acc.astype(o_ref.dtype)


def nm_matmul(x, g, sc, sh, w, out_dtype, tm, tn, rope=None):
    b, l, d = x.shape
    n = w.shape[1]
    assert l % tm == 0 and n % tn == 0
    in_specs = [
        pl.BlockSpec((1, tm, d), lambda bi, i, j: (bi, i, 0)),
        pl.BlockSpec((1, d), lambda bi, i, j: (0, 0)),
        pl.BlockSpec((1, 1, d), lambda bi, i, j: (bi, 0, 0)),
        pl.BlockSpec((1, 1, d), lambda bi, i, j: (bi, 0, 0)),
        pl.BlockSpec((d, tn), lambda bi, i, j: (0, j)),
    ]
    args = [x, g.reshape(1, d), sc, sh, w]
    rope_tiles = 0
    if rope is not None:
        cos_t, sin_t, rope_cols = rope
        rope_tiles = rope_cols // tn
        in_specs += [pl.BlockSpec((tm, tn), lambda bi, i, j: (i, 0))] * 2
        args += [cos_t, sin_t]
    return pl.pallas_call(
        functools.partial(_nm_matmul_kernel, rope_tiles=rope_tiles),
        grid=(b, l // tm, n // tn),
        in_specs=in_specs,
        out_specs=pl.BlockSpec((1, tm, tn), lambda bi, i, j: (bi, i, j)),
        out_shape=jax.ShapeDtypeStruct((b, l, n), out_dtype),
        scratch_shapes=[pltpu.VMEM((tm, d), BF16)],
        compiler_params=_cparams(("parallel", "parallel", "arbitrary")),
        name="nm_matmul",
    )(*args)


def _out_ffn_kernel(*refs, mode, final):
    it = iter(refs)
    x_ref = next(it)
    if mode == "glu":
        ys_ref, u_ref, dsk_ref = next(it), next(it), next(it)
    else:
        o_ref = next(it)
    wout_ref, g1_ref, gn_ref, sc_ref, sh_ref, g2_ref = (next(it) for _ in range(6))
    w1_ref, w3_ref, w2_ref = next(it), next(it), next(it)
    fg_ref = next(it) if final else None
    out_ref, x1_ref, h_ref, acc_ref = next(it), next(it), next(it), next(it)
    f = pl.program_id(2)
    d = x_ref.shape[-1]

    @pl.when(f == 0)
    def _():
        if mode == "glu":
            o = jax.nn.gelu(ys_ref[0] + dsk_ref[...] * u_ref[0]).astype(BF16)
            y2 = _dot(o, wout_ref[...])
            y = y2[:, :d] * jax.nn.sigmoid(y2[:, d:])
        else:
            y = _dot(o_ref[0], wout_ref[...])
        x1 = x_ref[0] + g1_ref[0] * y
        x1_ref[...] = x1
        h = _rms(x1, gn_ref[...]) * (1.0 + sc_ref[0]) + sh_ref[0]
        h_ref[...] = h.astype(BF16)
        acc_ref[...] = jnp.zeros_like(acc_ref)

    h = h_ref[...]
    a = _dot(h, w1_ref[...])
    b = _dot(h, w3_ref[...])
    t = (a * jax.nn.sigmoid(a) * b).astype(BF16)
    acc_ref[...] += _dot(t, w2_ref[...])

    @pl.when(f == pl.num_programs(2) - 1)
    def _():
        r = x1_ref[...] + g2_ref[0] * acc_ref[...]
        if final:
            r = _rms(r, fg_ref[...])
        out_ref[0] = r


def out_ffn(x, mix_in, wout, g1, gn, sc2, sh2, g2, w1, w3, w2, final_g, mode, tm, tf):
    b, l, d = x.shape
    fh = w1.shape[1]
    assert l % tm == 0 and fh % tf == 0
    tok = pl.BlockSpec((1, tm, d), lambda bi, i, f: (bi, i, 0))
    per_b = pl.BlockSpec((1, 1, d), lambda bi, i, f: (bi, 0, 0))
    vec = pl.BlockSpec((1, d), lambda bi, i, f: (0, 0))
    in_specs = [tok]
    args = [x]
    if mode == "glu":
        ys, u, dsk = mix_in
        in_specs += [tok, tok, vec]
        args += [ys, u, dsk.reshape(1, d)]
    else:
        in_specs += [tok]
        args += [mix_in]
    in_specs += [pl.BlockSpec(wout.shape, lambda bi, i, f: (0, 0)), per_b, vec, per_b, per_b, per_b,
                 pl.BlockSpec((d, tf), lambda bi, i, f: (0, f)),
                 pl.BlockSpec((d, tf), lambda bi, i, f: (0, f)),
                 pl.BlockSpec((tf, d), lambda bi, i, f: (f, 0))]
    args += [wout, g1, gn.reshape(1, d), sc2, sh2, g2, w1, w3, w2]
    final = final_g is not None
    if final:
        in_specs += [vec]
        args += [final_g.reshape(1, d)]
    return pl.pallas_call(
        functools.partial(_out_ffn_kernel, mode=mode, final=final),
        grid=(b, l // tm, fh // tf),
        in_specs=in_specs,
        out_specs=tok,
        out_shape=jax.ShapeDtypeStruct((b, l, d), F32),
        scratch_shapes=[pltpu.VMEM((tm, d), F32), pltpu.VMEM((tm, d), BF16), pltpu.VMEM((tm, d), F32)],
        compiler_params=_cparams(("parallel", "parallel", "arbitrary")),
        name="out_ffn",
    )(*args)


def _block_diag_rows(q):
    lane = lax.broadcasted_iota(jnp.int32, q.shape, 1)
    zero = jnp.zeros_like(q)
    return jnp.concatenate([jnp.where(lane < 64, q, zero), jnp.where(lane >= 64, q, zero)], axis=0)


def _da_kernel(lam_ref, g_ref, q_ref, k_ref, v_ref, o_ref, *, tq, kc, lambda_init):
    l = q_ref.shape[1]
    lam = lam_ref[...]
    lam_full = (jnp.exp(jnp.sum(lam[0:1] * lam[1:2], axis=-1, keepdims=True))
                - jnp.exp(jnp.sum(lam[2:3] * lam[3:4], axis=-1, keepdims=True)) + lambda_init)
    scale = DA_HEAD_DIM ** -0.5

    def q_block(qi, carry):
        q = q_ref[0, pl.ds(pl.multiple_of(qi * tq, tq), tq), :]
        qbd = _block_diag_rows(q * scale)

        def kv_step(ki, st):
            m, s_sum, acc = st
            ks = pl.ds(pl.multiple_of(ki * kc, kc), kc)
            s = _dot_nt(qbd, k_ref[0, ks, :])
            m_new = jnp.maximum(m, jnp.max(s, axis=-1, keepdims=True))
            alpha = jnp.exp(m - m_new)
            e = jnp.exp(s - m_new)
            s_sum = alpha * s_sum + jnp.sum(e, axis=-1, keepdims=True)
            acc = alpha * acc + _dot(e.astype(BF16), v_ref[0, ks, :])
            return m_new, s_sum, acc

        init = (jnp.full((2 * tq, 1), -1e30, F32), jnp.zeros((2 * tq, 1), F32),
                jnp.zeros((2 * tq, 2 * DA_HEAD_DIM), F32))
        _, s_sum, acc = lax.fori_loop(0, l // kc, kv_step, init)
        o = acc / s_sum
        a = o[:tq] - lam_full * o[tq:]
        r = _rms(a, g_ref[...]) * (1.0 - lambda_init)
        o_ref[0, pl.ds(pl.multiple_of(qi * tq, tq), tq), :] = r.astype(o_ref.dtype)
        return carry

    lax.fori_loop(0, l // tq, q_block, 0)


def diff_attention_core(qkv, lam, subln_g, lambda_init, tq, kc):
    b, l, n3 = qkv.shape
    d = n3 // 3
    hd = 2 * DA_HEAD_DIM
    nh = d // hd
    tq, kc = min(tq, l), min(kc, l)
    assert l % tq == 0 and l % kc == 0
    return pl.pallas_call(
        functools.partial(_da_kernel, tq=tq, kc=kc, lambda_init=lambda_init),
        grid=(b, nh),
        in_specs=[
            pl.BlockSpec(lam.shape, lambda bi, h: (0, 0)),
            pl.BlockSpec((1, hd), lambda bi, h: (0, 0)),
            pl.BlockSpec((1, l, hd), lambda bi, h: (bi, 0, h)),
            pl.BlockSpec((1, l, hd), lambda bi, h: (bi, 0, nh + h)),
            pl.BlockSpec((1, l, hd), lambda bi, h: (bi, 0, 2 * nh + h)),
        ],
        out_specs=pl.BlockSpec((1, l, hd), lambda bi, h: (bi, 0, h)),
        out_shape=jax.ShapeDtypeStruct((b, l, d), BF16),
        compiler_params=_cparams(("parallel", "parallel")),
        name="diff_attn",
    )(lam, subln_g.reshape(1, hd), qkv, qkv, qkv)


def _na_kernel(q_ref, k_ref, v_ref, t2_ref, o_ref, *, rows, pairs):
    kr = NA_WIN_ROWS
    kw = kr * GRID_W
    scale = NA_HEAD_DIM ** -0.5
    rr = lax.broadcasted_iota(jnp.int32, (2 * GRID_W, 2 * GRID_W), 0) % GRID_W
    cc = lax.broadcasted_iota(jnp.int32, (2 * GRID_W, 2 * GRID_W), 1) % GRID_W
    cs = jnp.clip(rr - NA_WIN_COLS // 2, 0, GRID_W - NA_WIN_COLS)
    valid2 = (cc >= cs) & (cc < cs + NA_WIN_COLS)
    valid = jnp.concatenate([valid2] * (kr // 2), axis=1)
    lane = lax.broadcasted_iota(jnp.int32, (GRID_W, 2 * NA_HEAD_DIM), 1)

    def row_step(r, carry):
        rs = jnp.clip(r - kr // 2, 0, rows - kr)
        rho0 = rs - r + NA_WIN_ROWS - 1
        qs = pl.ds(pl.multiple_of(r * GRID_W, GRID_W), GRID_W)
        bs = pl.ds(pl.multiple_of(rs * GRID_W, GRID_W), kw)
        for p in range(pairs):
            cols = slice(p * 128, (p + 1) * 128)
            qbd = _block_diag_rows(q_ref[0, qs, cols])
            s = _dot_nt(qbd, k_ref[0, bs, cols]) * scale
            bias = jnp.concatenate([t2_ref[p, rho0 + 2 * j] for j in range(kr // 2)], axis=1)
            s = jnp.where(valid, s + bias, -1e30)
            m = jnp.max(s, axis=-1, keepdims=True)
            e = jnp.exp(s - m)
            den = jnp.sum(e, axis=-1, keepdims=True)
            o = _dot(e.astype(BF16), v_ref[0, bs, cols]) / den
            o_ref[0, qs, cols] = jnp.where(lane < NA_HEAD_DIM, o[:GRID_W], o[GRID_W:]).astype(o_ref.dtype)
        return carry

    lax.fori_loop(0, rows, row_step, 0)


def na_bias_table(rpb):
    nh = rpb.shape[0]
    cols = jnp.arange(GRID_W)
    cidx = jnp.clip(cols[None, :] - cols[:, None], -(NA_WIN_COLS - 1), NA_WIN_COLS - 1) + NA_WIN_COLS - 1
    t = rpb[:, :, cidx]
    t = t.reshape(nh // 2, 2, 2 * NA_WIN_ROWS - 1, GRID_W, GRID_W).transpose(0, 2, 1, 3, 4)
    t = t.reshape(nh // 2, 2 * NA_WIN_ROWS - 1, 2 * GRID_W, GRID_W)
    return jnp.concatenate([t[:, :-1], t[:, 1:]], axis=-1).astype(F32)


def neighborhood_attention_core(qkv, rpb):
    b, l, n3 = qkv.shape
    d = n3 // 3
    rows = l // GRID_W
    assert rows >= NA_WIN_ROWS and l % GRID_W == 0
    pairs = 2
    gw = pairs * 2 * NA_HEAD_DIM
    ng = d // gw
    t2 = na_bias_table(rpb)
    nrho = t2.shape[1]
    return pl.pallas_call(
        functools.partial(_na_kernel, rows=rows, pairs=pairs),
        grid=(b, ng),
        in_specs=[
            pl.BlockSpec((1, l, gw), lambda bi, g: (bi, 0, g)),
            pl.BlockSpec((1, l, gw), lambda bi, g: (bi, 0, ng + g)),
            pl.BlockSpec((1, l, gw), lambda bi, g: (bi, 0, 2 * ng + g)),
            pl.BlockSpec((pairs, nrho, 2 * GRID_W, 2 * GRID_W), lambda bi, g: (g, 0, 0, 0)),
        ],
        out_specs=pl.BlockSpec((1, l, gw), lambda bi, g: (bi, 0, g)),
        out_shape=jax.ShapeDtypeStruct((b, l, d), BF16),
        compiler_params=_cparams(("parallel", "parallel")),
        name="nbr_attn",
    )(qkv, qkv, qkv, t2)


def s5_operators(a_re, a_im, log_dt, b_re, b_im, c_re, c_im):
    t = S5_CHUNK
    lam = lax.complex(a_re.astype(F32), a_im.astype(F32))
    dt = jnp.exp(log_dt.astype(F32))[..., None]
    lam_bar = jnp.exp(lam * dt)
    b_bar = ((lam_bar - 1.0) / lam)[..., None] * lax.complex(b_re.astype(F32), b_im.astype(F32))
    c_mat = lax.complex(c_re.astype(F32), c_im.astype(F32))
    taus = jnp.arange(t + 1, dtype=F32)
    pw = jnp.exp((lam * dt)[:, :, None, :] * taus[None, None, :, None])
    kern = jnp.real(jnp.einsum('dgpn,dgtn,dgnq->dgtpq', c_mat, pw[:, :, :t], b_bar, precision=HI))
    ti = jnp.arange(t)
    tau_f = ti[None, :] - ti[:, None]
    kf = jnp.where((tau_f >= 0)[None, :, :, None, None], kern[0][:, jnp.clip(tau_f, 0, t - 1)], 0.0)
    kb = jnp.where((tau_f <= 0)[None, :, :, None, None], kern[1][:, jnp.clip(-tau_f, 0, t - 1)], 0.0)
    g, p = kern.shape[1], kern.shape[3]
    wy_u = (kf + kb).transpose(0, 1, 4, 2, 3).reshape(g, t * p, t * p)

    def state_to_y(dr, powers):
        m = c_mat[dr][:, None, :, :] * pw[dr][:, powers, None, :]
        w = jnp.concatenate([jnp.real(m), -jnp.imag(m)], axis=-1)
        return w.transpose(0, 3, 1, 2).reshape(g, -1, t * p)

    wy_x = jnp.concatenate([state_to_y(0, ti + 1), state_to_y(1, t - ti)], axis=1)

    def u_to_state(dr, powers):
        m = pw[dr][:, powers, :, None] * b_bar[dr][:, None, :, :]
        w = jnp.concatenate([jnp.real(m), jnp.imag(m)], axis=2)
        return w.transpose(0, 1, 3, 2).reshape(g, t * p, -1)

    wb = jnp.concatenate([u_to_state(0, t - 1 - ti), u_to_state(1, ti)], axis=-1)
    a16 = pw[:, :, t, :]
    rot = jnp.stack([jnp.concatenate([jnp.real(a16), jnp.real(a16)], -1),
                     jnp.concatenate([-jnp.imag(a16), jnp.imag(a16)], -1)], axis=2)
    rot = rot.transpose(1, 0, 2, 3).reshape(g, 4, -1)
    return wb.astype(BF16), wy_u.astype(BF16), wy_x.astype(BF16), rot.astype(F32)


def _s5_kernel(u_ref, wb_ref, wyu_ref, wyx_ref, rot_ref, y_ref, s_ref, x_ref, *, nc, nb, gb):
    ns2 = 2 * S5_STATE
    for g in range(gb):
        s_ref[g] = _dot(u_ref[g], wb_ref[g])

    def step(c, xs):
        cb = nc - 1 - c
        rf = pl.ds(pl.multiple_of(c * nb, nb), nb)
        rb = pl.ds(pl.multiple_of(cb * nb, nb), nb)
        new = []
        for g in range(gb):
            xf, xb = xs[2 * g], xs[2 * g + 1]
            rot = rot_ref[g]
            x_ref[g, rf, 0:ns2] = xf.astype(BF16)
            x_ref[g, rb, ns2:2 * ns2] = xb.astype(BF16)
            xf = rot[0:1] * xf + rot[1:2] * pltpu.roll(xf, S5_STATE, 1) + s_ref[g, rf, 0:ns2]
            xb = rot[2:3] * xb + rot[3:4] * pltpu.roll(xb, S5_STATE, 1) + s_ref[g, rb, ns2:2 * ns2]
            new += [xf, xb]
        return tuple(new)

    lax.fori_loop(0, nc, step, tuple(jnp.zeros((nb, ns2), F32) for _ in range(2 * gb)))
    for g in range(gb):
        y_ref[g] = _dot(u_ref[g], wyu_ref[g]) + _dot(x_ref[g], wyx_ref[g])


def s5_core(u, ops, gb=2):
    b, l, d = u.shape
    t, p = S5_CHUNK, S5_GROUP
    g = d // p
    nc = l // t
    w = t * p
    wb, wy_u, wy_x, rot = ops
    ur = u.astype(BF16).reshape(b, nc, t, g, p).transpose(3, 1, 0, 2, 4).reshape(g, nc * b, w)
    grp = lambda shape: pl.BlockSpec((gb,) + shape, lambda i: (i, 0, 0))
    yr = pl.pallas_call(
        functools.partial(_s5_kernel, nc=nc, nb=b, gb=gb),
        grid=(g // gb,),
        in_specs=[grp((nc * b, w)), grp(wb.shape[1:]), grp(wy_u.shape[1:]), grp(wy_x.shape[1:]), grp(rot.shape[1:])],
        out_specs=grp((nc * b, w)),
        out_shape=jax.ShapeDtypeStruct((g, nc * b, w), F32),
        scratch_shapes=[pltpu.VMEM((gb, nc * b, 4 * S5_STATE), F32), pltpu.VMEM((gb, nc * b, 4 * S5_STATE), BF16)],
        compiler_params=_cparams(("parallel",)),
        name="s5_scan",
    )(ur, wb, wy_u, wy_x, rot)
    return yr.reshape(g, nc, b, t, p).transpose(2, 1, 3, 0, 4).reshape(b, l, d)


def _tri_inverse(m, eye):
    a = -m
    p = eye + a
    for _ in range(5):
        a = jnp.dot(a, a, preferred_element_type=F32, precision=HI)
        p = p + jnp.dot(p, a, preferred_element_type=F32, precision=HI)
    return p


def _dn_kernel(hp_ref, gn_ref, wq_ref, wk_ref, wv_ref, q_ref, k_ref, v_ref, z_ref, ab_ref, o_ref,
               u_ref, w_ref, qd_ref, kd_ref, at_ref, cd_ref, of_ref, ob_ref, *, nheads):
    l = q_ref.shape[1]
    c = DN_CHUNK
    n = l // c
    dk = DN_HEAD_DIM
    h = pl.program_id(1)
    lane_hp = lax.broadcasted_iota(jnp.int32, hp_ref.shape, 1)
    hp = jnp.sum(jnp.where(lane_hp == h, hp_ref[...], 0.0), axis=-1, keepdims=True)
    row = lax.broadcasted_iota(jnp.int32, (c, c), 0)
    col = lax.broadcasted_iota(jnp.int32, (c, c), 1)
    eye = (row == col).astype(F32)
    incl = (row >= col, row <= col)
    strict = (row > col, row < col)
    lane_ab = lax.broadcasted_iota(jnp.int32, (c, LANES), 1)
    ones_cc = jnp.ones((c, c), F32)

    def conv_silu(x_ref, w_ref_, ci):
        base = pl.multiple_of(ci * c, c)
        x = x_ref[0, pl.ds(base, c), :]
        prev = x_ref[0, pl.ds(jnp.maximum(base - 8, 0), 8), :] * (ci > 0).astype(F32)
        nxt = x_ref[0, pl.ds(jnp.minimum(base + c, l - 8), 8), :] * (ci < n - 1).astype(F32)
        e = jnp.concatenate([prev, x, nxt], axis=0)
        ne = c + 16
        w = w_ref_[...]
        y = (w[0:1] * pltpu.roll(e, 1, 0) + w[1:2] * e
             + w[2:3] * pltpu.roll(e, ne - 1, 0) + w[3:4] * pltpu.roll(e, ne - 2, 0))[8:8 + c]
        return y * jax.nn.sigmoid(y)

    def prep(ci, carry):
        rows = pl.ds(pl.multiple_of(ci * c, c), c)
        q = conv_silu(q_ref, wq_ref, ci)
        k = conv_silu(k_ref, wk_ref, ci)
        v = conv_silu(v_ref, wv_ref, ci)
        q = q * lax.rsqrt(jnp.sum(q * q, axis=-1, keepdims=True) + EPS) * (dk ** -0.5)
        k = k * lax.rsqrt(jnp.sum(k * k, axis=-1, keepdims=True) + EPS)
        ab = ab_ref[0, rows, :]
        pick = lambda j: jnp.sum(jnp.where(lane_ab == j * nheads + h, ab, 0.0), axis=-1, keepdims=True)
        kb16, q16 = k.astype(BF16), q.astype(BF16)
        qk = _dot_nt(q16, kb16)
        for dr in range(2):
            g = -jnp.exp(hp[dr:dr + 1]) * jax.nn.softplus(pick(dr) + hp[2 + dr:3 + dr])
            beta = jax.nn.sigmoid(pick(2 + dr))
            tri = incl[dr].astype(F32)
            gc = jnp.dot(tri, jnp.broadcast_to(g, (c, dk)), preferred_element_type=F32, precision=HI)
            gc_row = jnp.dot(ones_cc, jnp.where(incl[1 - dr], g, 0.0), preferred_element_type=F32, precision=HI)
            diff = gc[:, :c] - gc_row
            decay = jnp.where(incl[dr], jnp.exp(jnp.where(incl[dr], diff, 0.0)), 0.0)
            kbeta = k * beta
            m = jnp.where(strict[dr], _dot_nt(kbeta.astype(BF16), kb16) * decay, 0.0)
            egc = jnp.exp(gc)
            rhs = jnp.concatenate([v * beta, kbeta * egc], axis=1)
            sol = jnp.dot(_tri_inverse(m, eye), rhs, preferred_element_type=F32, precision=HI)
            gc_last = gc[c - 1:c] if dr == 0 else gc[0:1]
            u_ref[dr, rows, :] = sol[:, :dk]
            w_ref[dr, rows, :] = sol[:, dk:].astype(BF16)
            qd_ref[dr, rows, :] = (q * egc).astype(BF16)
            kd_ref[dr, rows, :] = (k * jnp.exp(gc_last - gc)).astype(BF16)
            at_ref[dr, rows, :] = (qk * decay).astype(BF16)
            cd_ref[dr, pl.ds(pl.multiple_of(ci * 8, 8), 8), :] = jnp.broadcast_to(jnp.exp(gc_last), (8, dk))
        return carry

    lax.fori_loop(0, n, prep, 0)

    def chain(dr, ci, s, out_ref):
        rows = pl.ds(pl.multiple_of(ci * c, c), c)
        s16 = s.astype(BF16)
        v_new = u_ref[dr, rows, :] - _dot(w_ref[dr, rows, :], s16)
        v16 = v_new.astype(BF16)
        out_ref[rows, :] = _dot(qd_ref[dr, rows, :], s16) + _dot(at_ref[dr, rows, :], v16)
        cd = cd_ref[dr, pl.ds(pl.multiple_of(ci * 8, 8), 1), :]
        return s * cd + _dot_tn(kd_ref[dr, rows, :], v16)

    def step(i, st):
        sf, sb = st
        return chain(0, i, sf, of_ref), chain(1, n - 1 - i, sb, ob_ref)

    zero = jnp.zeros((dk, dk), F32)
    lax.fori_loop(0, n, step, (zero, zero))

    def finish(ci, carry):
        rows = pl.ds(pl.multiple_of(ci * c, c), c)
        z = z_ref[0, rows, :]
        r = _rms(of_ref[rows, :] + ob_ref[rows, :], gn_ref[...]) * (z * jax.nn.sigmoid(z))
        o_ref[0, rows, :] = r.astype(o_ref.dtype)
        return carry

    lax.fori_loop(0, n, finish, 0)


def gated_deltanet_core(proj, conv_w, a_log, dt_bias, onorm_g):
    b, l, _ = proj.shape
    d = conv_w.shape[1] // 3
    dk = DN_HEAD_DIM
    nh = d // dk
    n = l // DN_CHUNK
    assert l % DN_CHUNK == 0 and dk == LANES
    hp = jnp.concatenate([a_log, dt_bias], axis=0).astype(F32)
    tok = lambda off: pl.BlockSpec((1, l, dk), lambda bi, h: (bi, 0, off + h))
    cw = lambda off: pl.BlockSpec((DN_CONV, dk), lambda bi, h: (0, off + h))
    return pl.pallas_call(
        functools.partial(_dn_kernel, nheads=nh),
        grid=(b, nh),
        in_specs=[
            pl.BlockSpec(hp.shape, lambda bi, h: (0, 0)),
            pl.BlockSpec((1, dk), lambda bi, h: (0, 0)),
            cw(0), cw(nh), cw(2 * nh),
            tok(0), tok(nh), tok(2 * nh), tok(3 * nh),
            pl.BlockSpec((1, l, dk), lambda bi, h: (bi, 0, 4 * nh)),
        ],
        out_specs=pl.BlockSpec((1, l, dk), lambda bi, h: (bi, 0, h)),
        out_shape=jax.ShapeDtypeStruct((b, l, d), BF16),
        scratch_shapes=[
            pltpu.VMEM((2, l, dk), F32), pltpu.VMEM((2, l, dk), BF16), pltpu.VMEM((2, l, dk), BF16),
            pltpu.VMEM((2, l, dk), BF16), pltpu.VMEM((2, l, DN_CHUNK), BF16), pltpu.VMEM((2, n * 8, dk), F32),
            pltpu.VMEM((l, dk), F32), pltpu.VMEM((l, dk), F32),
        ],
        compiler_params=_cparams(("parallel", "parallel")),
        name="gated_deltanet",
    )(hp, onorm_g.reshape(1, dk), conv_w, conv_w, conv_w, proj, proj, proj, proj, proj)


def _rope_tables(l, width):
    half = DA_HEAD_DIM // 2
    inv = jnp.power(ROPE_THETA, -jnp.arange(half, dtype=F32) * 2.0 / DA_HEAD_DIM)
    ang = jnp.arange(l, dtype=F32)[:, None] * inv[None, :]
    cos, sin = jnp.cos(ang), jnp.sin(ang)
    reps = width // DA_HEAD_DIM
    return (jnp.tile(jnp.concatenate([cos, cos], axis=-1), (1, reps)),
            jnp.tile(jnp.concatenate([-sin, sin], axis=-1), (1, reps)))


def _pad_cols(w, mult):
    pad = -w.shape[1] % mult
    return jnp.pad(w, ((0, 0), (0, pad))) if pad else w


def _layer(x, mod_i, p, i, final):
    b, l, d = x.shape
    tm = min(1024, l)
    tn = 512
    tm_ffn = min(512, l)
    tf = 256
    m, j = i % N_MIXERS, i // N_MIXERS
    sh1, sc1, g1, sh2, sc2, g2 = [mod_i[:, None, k * d:(k + 1) * d] for k in range(6)]
    n1 = p['norm1_g'][i]
    mode = "linear"
    if m == 0:
        lambda_init = 0.8 - 0.6 * math.exp(-0.3 * i)
        qkv = nm_matmul(x, n1, sc1, sh1, p['da_w_in'][j].astype(BF16), BF16, tm, tn,
                        rope=_rope_tables(l, tn) + (2 * d,))
        mix = diff_attention_core(qkv, p['da_lam'][j].astype(F32), p['da_subln_g'][j], lambda_init, 128, 512)
        wout = p['da_w_out'][j]
    elif m == 1:
        u = nm_matmul(x, n1, sc1, sh1, p['s5_w_in'][j].astype(BF16), F32, tm, tn)
        ops = s5_operators(p['s5_a_re'][j], p['s5_a_im'][j], p['s5_log_dt'][j], p['s5_b_re'][j],
                           p['s5_b_im'][j], p['s5_c_re'][j], p['s5_c_im'][j])
        mix = (s5_core(u, ops), u, p['s5_d'][j])
        wout = p['s5_w_glu'][j]
        mode = "glu"
    elif m == 2:
        qkv = nm_matmul(x, n1, sc1, sh1, p['na_w_in'][j].astype(BF16), BF16, tm, tn)
        mix = neighborhood_attention_core(qkv, p['na_rpb'][j])
        wout = p['na_w_out'][j]
    else:
        proj = nm_matmul(x, n1, sc1, sh1, _pad_cols(p['dn_w_in'][j], tn).astype(BF16), F32, tm, tn)
        mix = gated_deltanet_core(proj, p['dn_conv_w'][j], p['dn_a_log'][j], p['dn_dt_bias'][j],
                                  p['dn_onorm_g'][j])
        wout = p['dn_w_out'][j]
    return out_ffn(x, mix, wout.astype(BF16), g1, p['norm2_g'][i], sc2, sh2, g2,
                   p['ffn_w1'][i].astype(BF16), p['ffn_w3'][i].astype(BF16), p['ffn_w2'][i].astype(BF16),
                   p['final_g'] if final else None, mode, tm_ffn, tf)


def _encoder_trunk(x, mod, p):
    depth = p['ffn_w1'].shape[0]
    for i in range(depth):
        x = _layer(x, mod[i], p, i, i == depth - 1)
    return x


def kernel(x_prompt, x_sample, c_prompt, c_sample, ada_w, ada_b, norm1_g, norm2_g, ffn_w1, ffn_w3, ffn_w2, da_w_in, da_lam, da_subln_g, da_w_out, s5_w_in, s5_a_re, s5_a_im, s5_log_dt, s5_b_re, s5_b_im, s5_c_re, s5_c_im, s5_d, s5_w_glu, na_w_in, na_rpb, na_w_out, dn_w_in, dn_conv_w, dn_a_log, dn_dt_bias, dn_onorm_g, dn_w_out, final_g):
    p = dict(norm1_g=norm1_g, norm2_g=norm2_g, ffn_w1=ffn_w1, ffn_w3=ffn_w3, ffn_w2=ffn_w2,
             da_w_in=da_w_in, da_lam=da_lam, da_subln_g=da_subln_g, da_w_out=da_w_out,
             s5_w_in=s5_w_in, s5_a_re=s5_a_re, s5_a_im=s5_a_im, s5_log_dt=s5_log_dt,
             s5_b_re=s5_b_re, s5_b_im=s5_b_im, s5_c_re=s5_c_re, s5_c_im=s5_c_im, s5_d=s5_d, s5_w_glu=s5_w_glu,
             na_w_in=na_w_in, na_rpb=na_rpb, na_w_out=na_w_out,
             dn_w_in=dn_w_in, dn_conv_w=dn_conv_w, dn_a_log=dn_a_log, dn_dt_bias=dn_dt_bias,
             dn_onorm_g=dn_onorm_g, dn_w_out=dn_w_out, final_g=final_g)
    nb = x_prompt.shape[0]
    mod = adaln(jnp.concatenate([c_prompt, c_sample], axis=0), ada_w, ada_b)
    y_prompt = _encoder_trunk(x_prompt, mod[:, :nb], p)
    y_sample = _encoder_trunk(x_sample, mod[:, nb:], p)
    return (y_prompt, y_sample)
```

```python
import functools
import math

import jax
import jax.numpy as jnp
from jax import lax
from jax.experimental import pallas as pl
from jax.experimental.pallas import tpu as pltpu

F32 = jnp.float32
BF16 = jnp.bfloat16
HI = lax.Precision.HIGHEST

EPS = 1e-6
ROPE_THETA = 10000.0
N_MIXERS = 4
GRID_W = 64
DA_HEAD_DIM = 64
Q_BLOCK = 128
S5_GROUP = 16
S5_STATE = 64
S5_CHUNK = 16
NA_HEAD_DIM = 64
NA_WIN_ROWS = 8
NA_WIN_COLS = 16
DN_HEAD_DIM = 128
DN_CONV = 4
DN_CHUNK = 64

V7X_VMEM_LIMIT = 52 * 1024 * 1024
LANES = 128
MM_SUB_ROWS = 512
MM_SUB_COLS = 256


def _cparams(sem):
    return pltpu.CompilerParams(dimension_semantics=sem, vmem_limit_bytes=V7X_VMEM_LIMIT)


def _rms(xf, g):
    return xf * lax.rsqrt(jnp.mean(xf * xf, axis=-1, keepdims=True) + EPS) * g


def _dot(a, b):
    return jnp.dot(a, b, preferred_element_type=F32)


def _dot_nt(a, b):
    return lax.dot_general(a, b, (((1,), (1,)), ((), ())), preferred_element_type=F32)


def _dot_tn(a, b):
    return lax.dot_general(a, b, (((0,), (0,)), ((), ())), preferred_element_type=F32)


def _adaln_kernel(c_ref, w_ref, b_ref, o_ref):
    c = c_ref[...]
    a = c * jax.nn.sigmoid(c)
    o_ref[0] = jnp.dot(a, w_ref[0], preferred_element_type=F32, precision=HI) + b_ref[0]


def adaln(c, ada_w, ada_b):
    depth, d, n = ada_w.shape
    rows = c.shape[0]
    tn = 1536 if n % 1536 == 0 else n
    return pl.pallas_call(
        _adaln_kernel,
        grid=(depth, n // tn),
        in_specs=[
            pl.BlockSpec((rows, d), lambda i, j: (0, 0)),
            pl.BlockSpec((1, d, tn), lambda i, j: (i, 0, j)),
            pl.BlockSpec((1, 1, tn), lambda i, j: (i, 0, j)),
        ],
        out_specs=pl.BlockSpec((1, rows, tn), lambda i, j: (i, 0, j)),
        out_shape=jax.ShapeDtypeStruct((depth, rows, n), F32),
        compiler_params=_cparams(("parallel", "parallel")),
        name="adaln",
    )(c, ada_w, ada_b.reshape(depth, 1, n))


def _swap_halves(a, half):
    n = a.shape[-1]
    lane = lax.broadcasted_iota(jnp.int32, a.shape, a.ndim - 1)
    first = (lane % (2 * half)) < half
    return jnp.where(first, pltpu.roll(a, n - half, a.ndim - 1), pltpu.roll(a, half, a.ndim - 1))


def _nm_matmul_kernel(x_ref, g_ref, sc_ref, sh_ref, w_ref, *rest, rope_tiles):
    if rope_tiles:
        cos_ref, sin_ref, o_ref, h_ref = rest
    else:
        o_ref, h_ref = rest
    j = pl.program_id(2)

    @pl.when(j == 0)
    def _():
        h = _rms(x_ref[0], g_ref[...]) * (1.0 + sc_ref[0]) + sh_ref[0]
        h_ref[...] = h.astype(BF16)

    tm, tn = o_ref.shape[1], o_ref.shape[2]
    sm, sn = min(tm, MM_SUB_ROWS), min(tn, MM_SUB_COLS)

    def tiles(with_rope):
        for r0 in range(0, tm, sm):
            for c0 in range(0, tn, sn):
                rs, cs = slice(r0, r0 + sm), slice(c0, c0 + sn)
                acc = _dot(h_ref[rs, :], w_ref[:, cs])
                if with_rope:
                    acc = acc * cos_ref[rs, cs] + _swap_halves(acc, DA_HEAD_DIM // 2) * sin_ref[rs, cs]
                o_ref[0, rs, cs] = acc.astype(o_ref.dtype)

    if rope_tiles:
        pl.when(j < rope_tiles)(lambda: tiles(True))
        pl.when(j >= rope_tiles)(lambda: tiles(False))
    else:
        tiles(False)


def nm_matmul(x, g, sc, sh, w, out_dtype, tm, tn, rope=None):
    b, l, d = x.shape
    n = w.shape[1]
    assert l % tm == 0 and n % tn == 0
    in_specs = [
        pl.BlockSpec((1, tm, d), lambda bi, i, j: (bi, i, 0)),
        pl.BlockSpec((1, d), lambda bi, i, j: (0, 0)),
        pl.BlockSpec((1, 1, d), lambda bi, i, j: (bi, 0, 0)),
        pl.BlockSpec((1, 1, d), lambda bi, i, j: (bi, 0, 0)),
        pl.BlockSpec((d, tn), lambda bi, i, j: (0, j)),
    ]
    args = [x, g.reshape(1, d), sc, sh, w]
    rope_tiles = 0
    if rope is not None:
        cos_t, sin_t, rope_cols = rope
        rope_tiles = rope_cols // tn
        in_specs += [pl.BlockSpec((tm, tn), lambda bi, i, j: (i, 0))] * 2
        args += [cos_t, sin_t]
    return pl.pallas_call(
        functools.partial(_nm_matmul_kernel, rope_tiles=rope_tiles),
        grid=(b, l // tm, n // tn),
        in_specs=in_specs,
        out_specs=pl.BlockSpec((1, tm, tn), lambda bi, i, j: (bi, i, j)),
        out_shape=jax.ShapeDtypeStruct((b, l, n), out_dtype),
        scratch_shapes=[pltpu.VMEM((tm, d), BF16)],
        compiler_params=_cparams(("parallel", "parallel", "arbitrary")),
        name="nm_matmul",
    )(*args)


def _out_ffn_kernel(*refs, mode, final):
    it = iter(refs)
    x_ref = next(it)
    if mode == "glu":
        ys_ref, u_ref, dsk_ref = next(it), next(it), next(it)
    else:
        o_ref = next(it)
    wout_ref, g1_ref, gn_ref, sc_ref, sh_ref, g2_ref = (next(it) for _ in range(6))
    w1_ref, w3_ref, w2_ref = next(it), next(it), next(it)
    fg_ref = next(it) if final else None
    out_ref, x1_ref, h_ref, acc_ref = next(it), next(it), next(it), next(it)
    f = pl.program_id(2)
    d = x_ref.shape[-1]

    tm = x_ref.shape[1]
    sm, sn = min(tm, MM_SUB_ROWS), min(d, MM_SUB_COLS)
    row_blocks = [slice(r0, r0 + sm) for r0 in range(0, tm, sm)]

    @pl.when(f == 0)
    def _():
        for rs in row_blocks:
            if mode == "glu":
                o = jax.nn.gelu(ys_ref[0, rs, :] + dsk_ref[...] * u_ref[0, rs, :]).astype(BF16)
            else:
                o = o_ref[0, rs, :]
            for c0 in range(0, d, sn):
                cs = slice(c0, c0 + sn)
                y = _dot(o, wout_ref[:, cs])
                if mode == "glu":
                    y = y * jax.nn.sigmoid(_dot(o, wout_ref[:, slice(d + c0, d + c0 + sn)]))
                x1_ref[rs, cs] = x_ref[0, rs, cs] + g1_ref[0, :, cs] * y
            h = _rms(x1_ref[rs, :], gn_ref[...]) * (1.0 + sc_ref[0]) + sh_ref[0]
            h_ref[rs, :] = h.astype(BF16)
        acc_ref[...] = jnp.zeros_like(acc_ref)

    for rs in row_blocks:
        h = h_ref[rs, :]
        a = _dot(h, w1_ref[...])
        b = _dot(h, w3_ref[...])
        t = (a * jax.nn.sigmoid(a) * b).astype(BF16)
        acc_ref[rs, :] += _dot(t, w2_ref[...])

    @pl.when(f == pl.num_programs(2) - 1)
    def _():
        r = x1_ref[...] + g2_ref[0] * acc_ref[...]
        if final:
            r = _rms(r, fg_ref[...])
        out_ref[0] = r


def out_ffn(x, mix_in, wout, g1, gn, sc2, sh2, g2, w1, w3, w2, final_g, mode, tm, tf):
    b, l, d = x.shape
    fh = w1.shape[1]
    assert l % tm == 0 and fh % tf == 0
    tok = pl.BlockSpec((1, tm, d), lambda bi, i, f: (bi, i, 0))
    per_b = pl.BlockSpec((1, 1, d), lambda bi, i, f: (bi, 0, 0))
    vec = pl.BlockSpec((1, d), lambda bi, i, f: (0, 0))
    in_specs = [tok]
    args = [x]
    if mode == "glu":
        ys, u, dsk = mix_in
        in_specs += [tok, tok, vec]
        args += [ys, u, dsk.reshape(1, d)]
    else:
        in_specs += [tok]
        args += [mix_in]
    in_specs += [pl.BlockSpec(wout.shape, lambda bi, i, f: (0, 0)), per_b, vec, per_b, per_b, per_b,
                 pl.BlockSpec((d, tf), lambda bi, i, f: (0, f)),
                 pl.BlockSpec((d, tf), lambda bi, i, f: (0, f)),
                 pl.BlockSpec((tf, d), lambda bi, i, f: (f, 0))]
    args += [wout, g1, gn.reshape(1, d), sc2, sh2, g2, w1, w3, w2]
    final = final_g is not None
    if final:
        in_specs += [vec]
        args += [final_g.reshape(1, d)]
    return pl.pallas_call(
        functools.partial(_out_ffn_kernel, mode=mode, final=final),
        grid=(b, l // tm, fh // tf),
        in_specs=in_specs,
        out_specs=tok,
        out_shape=jax.ShapeDtypeStruct((b, l, d), F32),
        scratch_shapes=[pltpu.VMEM((tm, d), F32), pltpu.VMEM((tm, d), BF16), pltpu.VMEM((tm, d), F32)],
        compiler_params=_cparams(("parallel", "parallel", "arbitrary")),
        name="out_ffn",
    )(*args)


def _block_diag_rows(q):
    lane = lax.broadcasted_iota(jnp.int32, q.shape, 1)
    zero = jnp.zeros_like(q)
    return jnp.concatenate([jnp.where(lane < 64, q, zero), jnp.where(lane >= 64, q, zero)], axis=0)


def _da_kernel(lam_ref, g_ref, q_ref, k_ref, v_ref, o_ref, *, tq, kc, lambda_init):
    l = q_ref.shape[1]
    lam = lam_ref[...]
    lam_full = (jnp.exp(jnp.sum(lam[0:1] * lam[1:2], axis=-1, keepdims=True))
                - jnp.exp(jnp.sum(lam[2:3] * lam[3:4], axis=-1, keepdims=True)) + lambda_init)
    scale = DA_HEAD_DIM ** -0.5

    def q_block(qi, carry):
        q = q_ref[0, pl.ds(pl.multiple_of(qi * tq, tq), tq), :]
        qbd = _block_diag_rows(q * scale)

        nk = l // kc
        scores = lambda ki: _dot_nt(qbd, k_ref[0, pl.ds(pl.multiple_of(ki * kc, kc), kc), :])

        def kv_step(ki, st):
            s, m, s_sum, acc = st
            s_next = scores(jnp.minimum(ki + 1, nk - 1))
            m_new = jnp.maximum(m, jnp.max(s, axis=-1, keepdims=True))
            alpha = jnp.exp(m - m_new)
            e = jnp.exp(s - m_new)
            s_sum = alpha * s_sum + jnp.sum(e, axis=-1, keepdims=True)
            acc = alpha * acc + _dot(e.astype(BF16), v_ref[0, pl.ds(pl.multiple_of(ki * kc, kc), kc), :])
            return s_next, m_new, s_sum, acc

        init = (scores(0), jnp.full((2 * tq, 1), -1e30, F32), jnp.zeros((2 * tq, 1), F32),
                jnp.zeros((2 * tq, 2 * DA_HEAD_DIM), F32))
        _, _, s_sum, acc = lax.fori_loop(0, nk, kv_step, init, unroll=True)
        o = acc / s_sum
        a = o[:tq] - lam_full * o[tq:]
        r = _rms(a, g_ref[...]) * (1.0 - lambda_init)
        o_ref[0, pl.ds(pl.multiple_of(qi * tq, tq), tq), :] = r.astype(o_ref.dtype)
        return carry

    lax.fori_loop(0, l // tq, q_block, 0)


def diff_attention_core(qkv, lam, subln_g, lambda_init, tq, kc):
    b, l, n3 = qkv.shape
    d = n3 // 3
    hd = 2 * DA_HEAD_DIM
    nh = d // hd
    tq, kc = min(tq, l), min(kc, l)
    assert l % tq == 0 and l % kc == 0
    return pl.pallas_call(
        functools.partial(_da_kernel, tq=tq, kc=kc, lambda_init=lambda_init),
        grid=(b, nh),
        in_specs=[
            pl.BlockSpec(lam.shape, lambda bi, h: (0, 0)),
            pl.BlockSpec((1, hd), lambda bi, h: (0, 0)),
            pl.BlockSpec((1, l, hd), lambda bi, h: (bi, 0, h)),
            pl.BlockSpec((1, l, hd), lambda bi, h: (bi, 0, nh + h)),
            pl.BlockSpec((1, l, hd), lambda bi, h: (bi, 0, 2 * nh + h)),
        ],
        out_specs=pl.BlockSpec((1, l, hd), lambda bi, h: (bi, 0, h)),
        out_shape=jax.ShapeDtypeStruct((b, l, d), BF16),
        compiler_params=_cparams(("parallel", "parallel")),
        name="diff_attn",
    )(lam, subln_g.reshape(1, hd), qkv, qkv, qkv)


def _na_kernel(q_ref, k_ref, v_ref, t2_ref, o_ref, *, rows, pairs):
    kr = NA_WIN_ROWS
    kw = kr * GRID_W
    scale = NA_HEAD_DIM ** -0.5
    rr = lax.broadcasted_iota(jnp.int32, (2 * GRID_W, 2 * GRID_W), 0) % GRID_W
    cc = lax.broadcasted_iota(jnp.int32, (2 * GRID_W, 2 * GRID_W), 1) % GRID_W
    cs = jnp.clip(rr - NA_WIN_COLS // 2, 0, GRID_W - NA_WIN_COLS)
    valid2 = (cc >= cs) & (cc < cs + NA_WIN_COLS)
    valid = jnp.concatenate([valid2] * (kr // 2), axis=1)
    lane = lax.broadcasted_iota(jnp.int32, (GRID_W, 2 * NA_HEAD_DIM), 1)

    def row_step(r, carry):
        rs = jnp.clip(r - kr // 2, 0, rows - kr)
        rho0 = rs - r + NA_WIN_ROWS - 1
        qs = pl.ds(pl.multiple_of(r * GRID_W, GRID_W), GRID_W)
        bs = pl.ds(pl.multiple_of(rs * GRID_W, GRID_W), kw)
        for p in range(pairs):
            cols = slice(p * 128, (p + 1) * 128)
            qbd = _block_diag_rows(q_ref[0, qs, cols])
            s = _dot_nt(qbd, k_ref[0, bs, cols]) * scale
            bias = jnp.concatenate([t2_ref[p, rho0 + 2 * j] for j in range(kr // 2)], axis=1)
            s = jnp.where(valid, s + bias, -1e30)
            m = jnp.max(s, axis=-1, keepdims=True)
            e = jnp.exp(s - m)
            den = jnp.sum(e, axis=-1, keepdims=True)
            o = _dot(e.astype(BF16), v_ref[0, bs, cols]) / den
            o_ref[0, qs, cols] = jnp.where(lane < NA_HEAD_DIM, o[:GRID_W], o[GRID_W:]).astype(o_ref.dtype)
        return carry

    lax.fori_loop(0, rows, row_step, 0)


def na_bias_table(rpb):
    nh = rpb.shape[0]
    cols = jnp.arange(GRID_W)
    cidx = jnp.clip(cols[None, :] - cols[:, None], -(NA_WIN_COLS - 1), NA_WIN_COLS - 1) + NA_WIN_COLS - 1
    t = rpb[:, :, cidx]
    t = t.reshape(nh // 2, 2, 2 * NA_WIN_ROWS - 1, GRID_W, GRID_W).transpose(0, 2, 1, 3, 4)
    t = t.reshape(nh // 2, 2 * NA_WIN_ROWS - 1, 2 * GRID_W, GRID_W)
    return jnp.concatenate([t[:, :-1], t[:, 1:]], axis=-1).astype(F32)


def neighborhood_attention_core(qkv, rpb):
    b, l, n3 = qkv.shape
    d = n3 // 3
    rows = l // GRID_W
    assert rows >= NA_WIN_ROWS and l % GRID_W == 0
    pairs = 2
    gw = pairs * 2 * NA_HEAD_DIM
    ng = d // gw
    t2 = na_bias_table(rpb)
    nrho = t2.shape[1]
    return pl.pallas_call(
        functools.partial(_na_kernel, rows=rows, pairs=pairs),
        grid=(b, ng),
        in_specs=[
            pl.BlockSpec((1, l, gw), lambda bi, g: (bi, 0, g)),
            pl.BlockSpec((1, l, gw), lambda bi, g: (bi, 0, ng + g)),
            pl.BlockSpec((1, l, gw), lambda bi, g: (bi, 0, 2 * ng + g)),
            pl.BlockSpec((pairs, nrho, 2 * GRID_W, 2 * GRID_W), lambda bi, g: (g, 0, 0, 0)),
        ],
        out_specs=pl.BlockSpec((1, l, gw), lambda bi, g: (bi, 0, g)),
        out_shape=jax.ShapeDtypeStruct((b, l, d), BF16),
        compiler_params=_cparams(("parallel", "parallel")),
        name="nbr_attn",
    )(qkv, qkv, qkv, t2)


def s5_operators(a_re, a_im, log_dt, b_re, b_im, c_re, c_im):
    t = S5_CHUNK
    lam = lax.complex(a_re.astype(F32), a_im.astype(F32))
    dt = jnp.exp(log_dt.astype(F32))[..., None]
    lam_bar = jnp.exp(lam * dt)
    b_bar = ((lam_bar - 1.0) / lam)[..., None] * lax.complex(b_re.astype(F32), b_im.astype(F32))
    c_mat = lax.complex(c_re.astype(F32), c_im.astype(F32))
    taus = jnp.arange(t + 1, dtype=F32)
    pw = jnp.exp((lam * dt)[:, :, None, :] * taus[None, None, :, None])
    kern = jnp.real(jnp.einsum('dgpn,dgtn,dgnq->dgtpq', c_mat, pw[:, :, :t], b_bar, precision=HI))
    ti = jnp.arange(t)
    tau_f = ti[None, :] - ti[:, None]
    kf = jnp.where((tau_f >= 0)[None, :, :, None, None], kern[0][:, jnp.clip(tau_f, 0, t - 1)], 0.0)
    kb = jnp.where((tau_f <= 0)[None, :, :, None, None], kern[1][:, jnp.clip(-tau_f, 0, t - 1)], 0.0)
    g, p = kern.shape[1], kern.shape[3]
    wy_u = (kf + kb).transpose(0, 1, 4, 2, 3).reshape(g, t * p, t * p)

    def state_to_y(dr, powers):
        m = c_mat[dr][:, None, :, :] * pw[dr][:, powers, None, :]
        w = jnp.concatenate([jnp.real(m), -jnp.imag(m)], axis=-1)
        return w.transpose(0, 3, 1, 2).reshape(g, -1, t * p)

    wy_x = jnp.concatenate([state_to_y(0, ti + 1), state_to_y(1, t - ti)], axis=1)

    def u_to_state(dr, powers):
        m = pw[dr][:, powers, :, None] * b_bar[dr][:, None, :, :]
        w = jnp.concatenate([jnp.real(m), jnp.imag(m)], axis=2)
        return w.transpose(0, 1, 3, 2).reshape(g, t * p, -1)

    wb = jnp.concatenate([u_to_state(0, t - 1 - ti), u_to_state(1, ti)], axis=-1)
    a16 = pw[:, :, t, :]
    rot = jnp.stack([jnp.concatenate([jnp.real(a16), jnp.real(a16)], -1),
                     jnp.concatenate([-jnp.imag(a16), jnp.imag(a16)], -1)], axis=2)
    rot = rot.transpose(1, 0, 2, 3).reshape(g, 4, -1)
    return wb.astype(BF16), wy_u.astype(BF16), wy_x.astype(BF16), rot.astype(F32)


def _s5_kernel(u_ref, wb_ref, wyu_ref, wyx_ref, rot_ref, y_ref, s_ref, x_ref, *, nc, nb, gb):
    ns2 = 2 * S5_STATE
    for g in range(gb):
        s_ref[g] = _dot(u_ref[g], wb_ref[g])

    def step(c, xs):
        cb = nc - 1 - c
        rf = pl.ds(pl.multiple_of(c * nb, nb), nb)
        rb = pl.ds(pl.multiple_of(cb * nb, nb), nb)
        new = []
        for g in range(gb):
            xf, xb = xs[2 * g], xs[2 * g + 1]
            rot = rot_ref[g]
            x_ref[g, rf, 0:ns2] = xf.astype(BF16)
            x_ref[g, rb, ns2:2 * ns2] = xb.astype(BF16)
            xf = rot[0:1] * xf + rot[1:2] * pltpu.roll(xf, S5_STATE, 1) + s_ref[g, rf, 0:ns2]
            xb = rot[2:3] * xb + rot[3:4] * pltpu.roll(xb, S5_STATE, 1) + s_ref[g, rb, ns2:2 * ns2]
            new += [xf, xb]
        return tuple(new)

    lax.fori_loop(0, nc, step, tuple(jnp.zeros((nb, ns2), F32) for _ in range(2 * gb)))
    for g in range(gb):
        y_ref[g] = _dot(u_ref[g], wyu_ref[g]) + _dot(x_ref[g], wyx_ref[g])


def s5_core(u, ops, gb=2):
    b, l, d = u.shape
    t, p = S5_CHUNK, S5_GROUP
    g = d // p
    nc = l // t
    w = t * p
    wb, wy_u, wy_x, rot = ops
    ur = u.astype(BF16).reshape(b, nc, t, g, p).transpose(3, 1, 0, 2, 4).reshape(g, nc * b, w)
    grp = lambda shape: pl.BlockSpec((gb,) + shape, lambda i: (i, 0, 0))
    yr = pl.pallas_call(
        functools.partial(_s5_kernel, nc=nc, nb=b, gb=gb),
        grid=(g // gb,),
        in_specs=[grp((nc * b, w)), grp(wb.shape[1:]), grp(wy_u.shape[1:]), grp(wy_x.shape[1:]), grp(rot.shape[1:])],
        out_specs=grp((nc * b, w)),
        out_shape=jax.ShapeDtypeStruct((g, nc * b, w), F32),
        scratch_shapes=[pltpu.VMEM((gb, nc * b, 4 * S5_STATE), F32), pltpu.VMEM((gb, nc * b, 4 * S5_STATE), BF16)],
        compiler_params=_cparams(("parallel",)),
        name="s5_scan",
    )(ur, wb, wy_u, wy_x, rot)
    return yr.reshape(g, nc, b, t, p).transpose(2, 1, 3, 0, 4).reshape(b, l, d)


def _split2(a):
    hi = a.astype(BF16)
    return hi, (a - hi.astype(F32)).astype(BF16)


def _dot3(a, b):
    (ah, al), (bh, bl) = a, b
    return _dot(ah, bh) + (_dot(ah, bl) + _dot(al, bh))


def _tri_inverse_minus_eye(ms):
    n = [-m for m in ms]
    sa = [_split2(x) for x in n]
    for _ in range(5):
        a = [_dot3(x, x) for x in sa]
        sa = [_split2(x) for x in a]
        n = [y + x + _dot3(_split2(y), sx) for x, y, sx in zip(a, n, sa)]
    return n


def _dn_kernel(hp_ref, gn_ref, wq_ref, wk_ref, wv_ref, q_ref, k_ref, v_ref, z_ref, ab_ref, o_ref,
               u_ref, wq2_ref, kd_ref, at_ref, cd_ref, of_ref, ob_ref, *, nheads, unroll):
    l = q_ref.shape[1]
    c = DN_CHUNK
    c2 = 2 * c
    n = l // c
    dk = DN_HEAD_DIM
    h = pl.program_id(1)
    lane_hp = lax.broadcasted_iota(jnp.int32, hp_ref.shape, 1)
    hp = jnp.sum(jnp.where(lane_hp == h, hp_ref[...], 0.0), axis=-1, keepdims=True)
    row = lax.broadcasted_iota(jnp.int32, (c2, c2), 0)
    col = lax.broadcasted_iota(jnp.int32, (c2, c2), 1)
    fwd = row < c
    same = fwd == (col < c)
    incl = same & ((fwd & (row >= col)) | (jnp.logical_not(fwd) & (row <= col)))
    strict = incl & (row != col)
    tri16 = incl.astype(F32).astype(BF16)
    fwd_col = fwd[:, 0:1]
    a_log = jnp.where(fwd_col, hp[0:1], hp[1:2])
    dt_b = jnp.where(fwd_col, hp[2:3], hp[3:4])
    lane_ab = lax.broadcasted_iota(jnp.int32, (c, LANES), 1)

    def conv_silu(x_ref, w_ref_, ci):
        base = pl.multiple_of(ci * c, c)
        x = x_ref[0, pl.ds(base, c), :]
        prev = x_ref[0, pl.ds(jnp.maximum(base - 8, 0), 8), :] * (ci > 0).astype(F32)
        nxt = x_ref[0, pl.ds(jnp.minimum(base + c, l - 8), 8), :] * (ci < n - 1).astype(F32)
        e = jnp.concatenate([prev, x, nxt], axis=0)
        ne = c + 16
        w = w_ref_[...]
        y = (w[0:1] * pltpu.roll(e, 1, 0) + w[1:2] * e
             + w[2:3] * pltpu.roll(e, ne - 1, 0) + w[3:4] * pltpu.roll(e, ne - 2, 0))[8:8 + c]
        return y * jax.nn.sigmoid(y)

    def pick(ci, j):
        ab = ab_ref[0, pl.ds(pl.multiple_of(ci * c, c), c), :]
        return jnp.sum(jnp.where(lane_ab == j * nheads + h, ab, 0.0), axis=-1, keepdims=True)

    lane = lax.broadcasted_iota(jnp.int32, (c2, dk), 1)

    def prep(it, carry):
        cis = [it * unroll + j for j in range(unroll)]
        two = lambda x: jnp.concatenate([x, x], axis=0)
        q = [conv_silu(q_ref, wq_ref, ci) for ci in cis]
        k = [conv_silu(k_ref, wk_ref, ci) for ci in cis]
        v = [two(conv_silu(v_ref, wv_ref, ci)) for ci in cis]
        q = [two(x * lax.rsqrt(jnp.sum(x * x, axis=-1, keepdims=True) + EPS) * (dk ** -0.5)) for x in q]
        k = [two(x * lax.rsqrt(jnp.sum(x * x, axis=-1, keepdims=True) + EPS)) for x in k]
        g_in = [jnp.concatenate([pick(ci, 0), pick(ci, 1)], axis=0) for ci in cis]
        beta = [jax.nn.sigmoid(jnp.concatenate([pick(ci, 2), pick(ci, 3)], axis=0)) for ci in cis]
        g = [-jnp.exp(a_log) * jax.nn.softplus(x + dt_b) for x in g_in]
        g_hi = [x.astype(BF16).astype(F32) for x in g]
        r1 = [x - y for x, y in zip(g, g_hi)]
        g_mid = [x.astype(BF16).astype(F32) for x in r1]
        g3 = [jnp.where(lane == 0, a_, jnp.where(lane == 1, b_, jnp.where(lane == 2, r_ - b_, 0.0)))
              for a_, b_, r_ in zip(g_hi, g_mid, r1)]
        gc = [jnp.broadcast_to(jnp.sum(_dot(tri16, x.astype(BF16)), axis=-1, keepdims=True), (c2, dk)) for x in g3]
        decay = [jnp.where(incl, jnp.exp(jnp.where(incl, x - x.T, 0.0)), 0.0) for x in gc]
        k16 = [x.astype(BF16) for x in k]
        kbeta = [x * y for x, y in zip(k, beta)]
        m = [jnp.where(strict, _dot_nt(x.astype(BF16), y) * z, 0.0) for x, y, z in zip(kbeta, k16, decay)]
        egc = [jnp.exp(x) for x in gc]
        rhs = [jnp.concatenate([x * y, z * w_], axis=1) for x, y, z, w_ in zip(v, beta, kbeta, egc)]
        ninv = _tri_inverse_minus_eye(m)
        sol = [x + _dot(y.astype(BF16), x.astype(BF16)) for x, y in zip(rhs, ninv)]
        attn = [(_dot_nt(x.astype(BF16), y) * z).astype(BF16) for x, y, z in zip(q, k16, decay)]
        for j, ci in enumerate(cis):
            gcj = gc[j]
            gc_last = jnp.where(fwd, gcj[c - 1:c], gcj[c:c + 1])
            qd = (q[j] * egc[j]).astype(BF16)
            w16 = sol[j][:, dk:].astype(BF16)
            kd = (k[j] * jnp.exp(gc_last - gcj)).astype(BF16)
            r1_ = pl.ds(pl.multiple_of(ci * c, c), c)
            r2_ = pl.ds(pl.multiple_of(ci * c2, c2), c2)
            r8_ = pl.ds(pl.multiple_of(ci * 8, 8), 8)
            for dr, half in enumerate((slice(0, c), slice(c, c2))):
                u_ref[dr, r1_, :] = sol[j][half, :dk]
                wq2_ref[dr, r2_, :] = jnp.concatenate([w16[half], qd[half]], axis=0)
                kd_ref[dr, r1_, :] = kd[half]
                at_ref[dr, r1_, :] = attn[j][half]
            cd_ref[0, r8_, :] = jnp.broadcast_to(jnp.exp(gcj[c - 1:c]), (8, dk))
            cd_ref[1, r8_, :] = jnp.broadcast_to(jnp.exp(gcj[c:c + 1]), (8, dk))
        return carry

    lax.fori_loop(0, n // unroll, prep, 0)

    def step(i, st):
        sf, sb = st
        ib = n - 1 - i
        rf = pl.ds(pl.multiple_of(i * c, c), c)
        rb = pl.ds(pl.multiple_of(ib * c, c), c)
        sf16, sb16 = sf.astype(BF16), sb.astype(BF16)
        pf = _dot(wq2_ref[0, pl.ds(pl.multiple_of(i * c2, c2), c2), :], sf16)
        pb = _dot(wq2_ref[1, pl.ds(pl.multiple_of(ib * c2, c2), c2), :], sb16)
        v_new = jnp.concatenate([u_ref[0, rf, :] - pf[:c], u_ref[1, rb, :] - pb[:c]], axis=0)
        v16 = v_new.astype(BF16)
        attn = jnp.concatenate([at_ref[0, rf, :], at_ref[1, rb, :]], axis=0)
        o = _dot(attn, v16) + jnp.concatenate([pf[c:], pb[c:]], axis=0)
        of_ref[rf, :] = o[:c]
        ob_ref[rb, :] = o[c:]
        cdf = cd_ref[0, pl.ds(pl.multiple_of(i * 8, 8), 1), :]
        cdb = cd_ref[1, pl.ds(pl.multiple_of(ib * 8, 8), 1), :]
        sf = sf * cdf + _dot_tn(kd_ref[0, rf, :], v16[:c])
        sb = sb * cdb + _dot_tn(kd_ref[1, rb, :], v16[c:])
        return sf, sb

    zero = jnp.zeros((dk, dk), F32)
    lax.fori_loop(0, n, step, (zero, zero))

    def finish(ci, carry):
        rows = pl.ds(pl.multiple_of(ci * c, c), c)
        z = z_ref[0, rows, :]
        r = _rms(of_ref[rows, :] + ob_ref[rows, :], gn_ref[...]) * (z * jax.nn.sigmoid(z))
        o_ref[0, rows, :] = r.astype(o_ref.dtype)
        return carry

    lax.fori_loop(0, n, finish, 0)


def gated_deltanet_core(proj, conv_w, a_log, dt_bias, onorm_g):
    b, l, _ = proj.shape
    d = conv_w.shape[1] // 3
    dk = DN_HEAD_DIM
    nh = d // dk
    n = l // DN_CHUNK
    assert l % DN_CHUNK == 0 and dk == LANES == 2 * DN_CHUNK
    unroll = 4 if n % 4 == 0 else 1
    hp =jnp.concatenate([a_log, dt_bias], axis=0).astype(F32)
    tok = lambda off: pl.BlockSpec((1, l, dk), lambda bi, h: (bi, 0, off + h))
    cw = lambda off: pl.BlockSpec((DN_CONV, dk), lambda bi, h: (0, off + h))
    return pl.pallas_call(
        functools.partial(_dn_kernel, nheads=nh, unroll=unroll),
        grid=(b, nh),
        in_specs=[
            pl.BlockSpec(hp.shape, lambda bi, h: (0, 0)),
            pl.BlockSpec((1, dk), lambda bi, h: (0, 0)),
            cw(0), cw(nh), cw(2 * nh),
            tok(0), tok(nh), tok(2 * nh), tok(3 * nh),
            pl.BlockSpec((1, l, dk), lambda bi, h: (bi, 0, 4 * nh)),
        ],
        out_specs=pl.BlockSpec((1, l, dk), lambda bi, h: (bi, 0, h)),
        out_shape=jax.ShapeDtypeStruct((b, l, d), BF16),
        scratch_shapes=[
            pltpu.VMEM((2, l, dk), F32), pltpu.VMEM((2, 2 * l, dk), BF16), pltpu.VMEM((2, l, dk), BF16),
            pltpu.VMEM((2, l, 2 * DN_CHUNK), BF16), pltpu.VMEM((2, n * 8, dk), F32),
            pltpu.VMEM((l, dk), F32), pltpu.VMEM((l, dk), F32),
        ],
        compiler_params=_cparams(("parallel", "parallel")),
        name="gated_deltanet",
    )(hp, onorm_g.reshape(1, dk), conv_w, conv_w, conv_w, proj, proj, proj, proj, proj)


def _rope_tables(l, width):
    half = DA_HEAD_DIM // 2
    inv = jnp.power(ROPE_THETA, -jnp.arange(half, dtype=F32) * 2.0 / DA_HEAD_DIM)
    ang = jnp.arange(l, dtype=F32)[:, None] * inv[None, :]
    cos, sin = jnp.cos(ang), jnp.sin(ang)
    reps = width // DA_HEAD_DIM
    return (jnp.tile(jnp.concatenate([cos, cos], axis=-1), (1, reps)),
            jnp.tile(jnp.concatenate([-sin, sin], axis=-1), (1, reps)))


def _pad_cols(w, mult):
    pad = -w.shape[1] % mult
    return jnp.pad(w, ((0, 0), (0, pad))) if pad else w


def _layer(x, mod_i, p, i, final):
    b, l, d = x.shape
    tm = min(1024, l)
    tn = 512
    tm_ffn = min(512, l)
    tf = 256
    m, j = i % N_MIXERS, i // N_MIXERS
    sh1, sc1, g1, sh2, sc2, g2 = [mod_i[:, None, k * d:(k + 1) * d] for k in range(6)]
    n1 = p['norm1_g'][i]
    mode = "linear"
    if m == 0:
        lambda_init = 0.8 - 0.6 * math.exp(-0.3 * i)
        qkv = nm_matmul(x, n1, sc1, sh1, p['da_w_in'][j].astype(BF16), BF16, tm, tn,
                        rope=_rope_tables(l, tn) + (2 * d,))
        mix = diff_attention_core(qkv, p['da_lam'][j].astype(F32), p['da_subln_g'][j], lambda_init, 128, 512)
        wout = p['da_w_out'][j]
    elif m == 1:
        u = nm_matmul(x, n1, sc1, sh1, p['s5_w_in'][j].astype(BF16), F32, tm, tn)
        ops = s5_operators(p['s5_a_re'][j], p['s5_a_im'][j], p['s5_log_dt'][j], p['s5_b_re'][j],
                           p['s5_b_im'][j], p['s5_c_re'][j], p['s5_c_im'][j])
        mix = (s5_core(u, ops), u, p['s5_d'][j])
        wout = p['s5_w_glu'][j]
        mode = "glu"
    elif m == 2:
        qkv = nm_matmul(x, n1, sc1, sh1, p['na_w_in'][j].astype(BF16), BF16, tm, tn)
        mix = neighborhood_attention_core(qkv, p['na_rpb'][j])
        wout = p['na_w_out'][j]
    else:
        proj = nm_matmul(x, n1, sc1, sh1, _pad_cols(p['dn_w_in'][j], tn).astype(BF16), F32, tm, tn)
        mix = gated_deltanet_core(proj, p['dn_conv_w'][j], p['dn_a_log'][j], p['dn_dt_bias'][j],
                                  p['dn_onorm_g'][j])
        wout = p['dn_w_out'][j]
    return out_ffn(x, mix, wout.astype(BF16), g1, p['norm2_g'][i], sc2, sh2, g2,
                   p['ffn_w1'][i].astype(BF16), p['ffn_w3'][i].astype(BF16), p['ffn_w2'][i].astype(BF16),
                   p['final_g'] if final else None, mode, tm_ffn if mode == "glu" else tm, tf)


def _encoder_trunk(x, mod, p):
    depth = p['ffn_w1'].shape[0]
    for i in range(depth):
        x = _layer(x, mod[i], p, i, i == depth - 1)
    return x


def kernel(x_prompt, x_sample, c_prompt, c_sample, ada_w, ada_b, norm1_g, norm2_g, ffn_w1, ffn_w3, ffn_w2, da_w_in, da_lam, da_subln_g, da_w_out, s5_w_in, s5_a_re, s5_a_im, s5_log_dt, s5_b_re, s5_b_im, s5_c_re, s5_c_im, s5_d, s5_w_glu, na_w_in, na_rpb, na_w_out, dn_w_in, dn_conv_w, dn_a_log, dn_dt_bias, dn_onorm_g, dn_w_out, final_g):
    p = dict(norm1_g=norm1_g, norm2_g=norm2_g, ffn_w1=ffn_w1, ffn_w3=ffn_w3, ffn_w2=ffn_w2,
             da_w_in=da_w_in, da_lam=da_lam, da_subln_g=da_subln_g, da_w_out=da_w_out,
             s5_w_in=s5_w_in, s5_a_re=s5_a_re, s5_a_im=s5_a_im, s5_log_dt=s5_log_dt,
             s5_b_re=s5_b_re, s5_b_im=s5_b_im, s5_c_re=s5_c_re, s5_c_im=s5_c_im, s5_d=s5_d, s5_w_glu=s5_w_glu,
             na_w_in=na_w_in, na_rpb=na_rpb, na_w_out=na_w_out,
             dn_w_in=dn_w_in, dn_conv_w=dn_conv_w, dn_a_log=dn_a_log, dn_dt_bias=dn_dt_bias,
             dn_onorm_g=dn_onorm_g, dn_w_out=dn_w_out, final_g=final_g)
    nb = x_prompt.shape[0]
    mod = adaln(jnp.concatenate([c_prompt, c_sample], axis=0), ada_w, ada_b)
    y_prompt = _encoder_trunk(x_prompt, mod[:, :nb], p)
    y_sample = _encoder_trunk(x_sample, mod[:, nb:], p)
    return (y_prompt, y_sample)
```

```python
import functools
import math

import jax
import jax.numpy as jnp
from jax import lax
from jax.experimental import pallas as pl
from jax.experimental.pallas import tpu as pltpu

F32 = jnp.float32
BF16 = jnp.bfloat16
HI = lax.Precision.HIGHEST

EPS = 1e-6
ROPE_THETA = 10000.0
N_MIXERS = 4
GRID_W = 64
DA_HEAD_DIM = 64
Q_BLOCK = 128
S5_GROUP = 16
S5_STATE = 64
S5_CHUNK = 16
NA_HEAD_DIM = 64
NA_WIN_ROWS = 8
NA_WIN_COLS = 16
DN_HEAD_DIM = 128
DN_CONV = 4
DN_CHUNK = 64

V7X_VMEM_LIMIT = 52 * 1024 * 1024
LANES = 128
MM_SUB_ROWS = 512
MM_SUB_COLS = 256


def _cparams(sem):
    return pltpu.CompilerParams(dimension_semantics=sem, vmem_limit_bytes=V7X_VMEM_LIMIT)


def _rms(xf, g):
    return xf * lax.rsqrt(jnp.mean(xf * xf, axis=-1, keepdims=True) + EPS) * g


def _dot(a, b):
    return jnp.dot(a, b, preferred_element_type=F32)


def _dot_nt(a, b):
    return lax.dot_general(a, b, (((1,), (1,)), ((), ())), preferred_element_type=F32)


def _dot_tn(a, b):
    return lax.dot_general(a, b, (((0,), (0,)), ((), ())), preferred_element_type=F32)


def _adaln_kernel(c_ref, w_ref, b_ref, o_ref):
    c = c_ref[...]
    a = c * jax.nn.sigmoid(c)
    o_ref[0] = jnp.dot(a, w_ref[0], preferred_element_type=F32, precision=HI) + b_ref[0]


def adaln(c, ada_w, ada_b):
    depth, d, n = ada_w.shape
    rows = c.shape[0]
    tn = 1536 if n % 1536 == 0 else n
    return pl.pallas_call(
        _adaln_kernel,
        grid=(depth, n // tn),
        in_specs=[
            pl.BlockSpec((rows, d), lambda i, j: (0, 0)),
            pl.BlockSpec((1, d, tn), lambda i, j: (i, 0, j)),
            pl.BlockSpec((1, 1, tn), lambda i, j: (i, 0, j)),
        ],
        out_specs=pl.BlockSpec((1, rows, tn), lambda i, j: (i, 0, j)),
        out_shape=jax.ShapeDtypeStruct((depth, rows, n), F32),
        compiler_params=_cparams(("parallel", "parallel")),
        name="adaln",
    )(c, ada_w, ada_b.reshape(depth, 1, n))


def _swap_halves(a, half):
    n = a.shape[-1]
    lane = lax.broadcasted_iota(jnp.int32, a.shape, a.ndim - 1)
    first = (lane % (2 * half)) < half
    return jnp.where(first, pltpu.roll(a, n - half, a.ndim - 1), pltpu.roll(a, half, a.ndim - 1))


def _nm_matmul_kernel(x_ref, g_ref, sc_ref, sh_ref, w_ref, *rest, rope_tiles):
    if rope_tiles:
        cos_ref, sin_ref, o_ref, h_ref = rest
    else:
        o_ref, h_ref = rest
    j = pl.program_id(2)

    @pl.when(j == 0)
    def _():
        h = _rms(x_ref[0], g_ref[...]) * (1.0 + sc_ref[0]) + sh_ref[0]
        h_ref[...] = h.astype(BF16)

    tm, tn = o_ref.shape[1], o_ref.shape[2]
    sm, sn = min(tm, MM_SUB_ROWS), min(tn, MM_SUB_COLS)

    def tiles(with_rope):
        for r0 in range(0, tm, sm):
            for c0 in range(0, tn, sn):
                rs, cs = slice(r0, r0 + sm), slice(c0, c0 + sn)
                acc = _dot(h_ref[rs, :], w_ref[:, cs])
                if with_rope:
                    acc = acc * cos_ref[rs, cs] + _swap_halves(acc, DA_HEAD_DIM // 2) * sin_ref[rs, cs]
                o_ref[0, rs, cs] = acc.astype(o_ref.dtype)

    if rope_tiles:
        pl.when(j < rope_tiles)(lambda: tiles(True))
        pl.when(j >= rope_tiles)(lambda: tiles(False))
    else:
        tiles(False)


def nm_matmul(x, g, sc, sh, w, out_dtype, tm, tn, rope=None):
    b, l, d = x.shape
    n = w.shape[1]
    assert l % tm == 0 and n % tn == 0
    in_specs = [
        pl.BlockSpec((1, tm, d), lambda bi, i, j: (bi, i, 0)),
        pl.BlockSpec((1, d), lambda bi, i, j: (0, 0)),
        pl.BlockSpec((1, 1, d), lambda bi, i, j: (bi, 0, 0)),
        pl.BlockSpec((1, 1, d), lambda bi, i, j: (bi, 0, 0)),
        pl.BlockSpec((d, tn), lambda bi, i, j: (0, j)),
    ]
    args = [x, g.reshape(1, d), sc, sh, w]
    rope_tiles = 0
    if rope is not None:
        cos_t, sin_t, rope_cols = rope
        rope_tiles = rope_cols // tn
        in_specs += [pl.BlockSpec((tm, tn), lambda bi, i, j: (i, 0))] * 2
        args += [cos_t, sin_t]
    return pl.pallas_call(
        functools.partial(_nm_matmul_kernel, rope_tiles=rope_tiles),
        grid=(b, l // tm, n // tn),
        in_specs=in_specs,
        out_specs=pl.BlockSpec((1, tm, tn), lambda bi, i, j: (bi, i, j)),
        out_shape=jax.ShapeDtypeStruct((b, l, n), out_dtype),
        scratch_shapes=[pltpu.VMEM((tm, d), BF16)],
        compiler_params=_cparams(("parallel", "parallel", "arbitrary")),
        name="nm_matmul",
    )(*args)


def _out_ffn_kernel(*refs, mode, final):
    it = iter(refs)
    x_ref = next(it)
    if mode == "glu":
        ys_ref, u_ref, dsk_ref = next(it), next(it), next(it)
    else:
        o_ref = next(it)
    wout_ref, g1_ref, gn_ref, sc_ref, sh_ref, g2_ref = (next(it) for _ in range(6))
    w1_ref, w3_ref, w2_ref = next(it), next(it), next(it)
    fg_ref = next(it) if final else None
    out_ref, x1_ref, h_ref, acc_ref = next(it), next(it), next(it), next(it)
    f = pl.program_id(2)
    d = x_ref.shape[-1]

    tm = x_ref.shape[1]
    sm, sn = min(tm, MM_SUB_ROWS), min(d, MM_SUB_COLS)
    row_blocks = [slice(r0, r0 + sm) for r0 in range(0, tm, sm)]

    @pl.when(f == 0)
    def _():
        for rs in row_blocks:
            if mode == "glu":
                o = jax.nn.gelu(ys_ref[0, rs, :] + dsk_ref[...] * u_ref[0, rs, :]).astype(BF16)
            else:
                o = o_ref[0, rs, :]
            for c0 in range(0, d, sn):
                cs = slice(c0, c0 + sn)
                y = _dot(o, wout_ref[:, cs])
                if mode == "glu":
                    y = y * jax.nn.sigmoid(_dot(o, wout_ref[:, slice(d + c0, d + c0 + sn)]))
                x1_ref[rs, cs] = x_ref[0, rs, cs] + g1_ref[0, :, cs] * y
            h = _rms(x1_ref[rs, :], gn_ref[...]) * (1.0 + sc_ref[0]) + sh_ref[0]
            h_ref[rs, :] = h.astype(BF16)
        acc_ref[...] = jnp.zeros_like(acc_ref)

    for rs in row_blocks:
        h = h_ref[rs, :]
        a = _dot(h, w1_ref[...])
        b = _dot(h, w3_ref[...])
        t = (a * jax.nn.sigmoid(a) * b).astype(BF16)
        acc_ref[rs, :] += _dot(t, w2_ref[...])

    @pl.when(f == pl.num_programs(2) - 1)
    def _():
        r = x1_ref[...] + g2_ref[0] * acc_ref[...]
        if final:
            r = _rms(r, fg_ref[...])
        out_ref[0] = r


def out_ffn(x, mix_in, wout, g1, gn, sc2, sh2, g2, w1, w3, w2, final_g, mode, tm, tf):
    b, l, d = x.shape
    fh = w1.shape[1]
    assert l % tm == 0 and fh % tf == 0
    tok = pl.BlockSpec((1, tm, d), lambda bi, i, f: (bi, i, 0))
    per_b = pl.BlockSpec((1, 1, d), lambda bi, i, f: (bi, 0, 0))
    vec = pl.BlockSpec((1, d), lambda bi, i, f: (0, 0))
    in_specs = [tok]
    args = [x]
    if mode == "glu":
        ys, u, dsk = mix_in
        in_specs += [tok, tok, vec]
        args += [ys, u, dsk.reshape(1, d)]
    else:
        in_specs += [tok]
        args += [mix_in]
    in_specs += [pl.BlockSpec(wout.shape, lambda bi, i, f: (0, 0), pipeline_mode=pl.Buffered(1)),
                 per_b, vec, per_b, per_b, per_b,
                 pl.BlockSpec((d, tf), lambda bi, i, f: (0, f)),
                 pl.BlockSpec((d, tf), lambda bi, i, f: (0, f)),
                 pl.BlockSpec((tf, d), lambda bi, i, f: (f, 0))]
    args += [wout, g1, gn.reshape(1, d), sc2, sh2, g2, w1, w3, w2]
    final = final_g is not None
    if final:
        in_specs += [vec]
        args += [final_g.reshape(1, d)]
    return pl.pallas_call(
        functools.partial(_out_ffn_kernel, mode=mode, final=final),
        grid=(b, l // tm, fh // tf),
        in_specs=in_specs,
        out_specs=tok,
        out_shape=jax.ShapeDtypeStruct((b, l, d), F32),
        scratch_shapes=[pltpu.VMEM((tm, d), F32), pltpu.VMEM((tm, d), BF16), pltpu.VMEM((tm, d), F32)],
        compiler_params=_cparams(("parallel", "parallel", "arbitrary")),
        name="out_ffn",
    )(*args)


def _block_diag_rows(q):
    lane = lax.broadcasted_iota(jnp.int32, q.shape, 1)
    zero = jnp.zeros_like(q)
    return jnp.concatenate([jnp.where(lane < 64, q, zero), jnp.where(lane >= 64, q, zero)], axis=0)


def _da_kernel(lam_ref, g_ref, q_ref, k_ref, v_ref, o_ref, *, tq, kc, lambda_init):
    l = q_ref.shape[1]
    lam = lam_ref[...]
    lam_full = (jnp.exp(jnp.sum(lam[0:1] * lam[1:2], axis=-1, keepdims=True))
                - jnp.exp(jnp.sum(lam[2:3] * lam[3:4], axis=-1, keepdims=True)) + lambda_init)
    scale = DA_HEAD_DIM ** -0.5

    def q_block(qi, carry):
        q = q_ref[0, pl.ds(pl.multiple_of(qi * tq, tq), tq), :]
        qbd = _block_diag_rows(q * scale)

        nk = l // kc
        scores = lambda ki: _dot_nt(qbd, k_ref[0, pl.ds(pl.multiple_of(ki * kc, kc), kc), :])

        def kv_step(ki, st):
            s, m, s_sum, acc = st
            s_next = scores(jnp.minimum(ki + 1, nk - 1))
            m_new = jnp.maximum(m, jnp.max(s, axis=-1, keepdims=True))
            alpha = jnp.exp(m - m_new)
            e = jnp.exp(s - m_new)
            s_sum = alpha * s_sum + jnp.sum(e, axis=-1, keepdims=True)
            acc = alpha * acc + _dot(e.astype(BF16), v_ref[0, pl.ds(pl.multiple_of(ki * kc, kc), kc), :])
            return s_next, m_new, s_sum, acc

        init = (scores(0), jnp.full((2 * tq, 1), -1e30, F32), jnp.zeros((2 * tq, 1), F32),
                jnp.zeros((2 * tq, 2 * DA_HEAD_DIM), F32))
        _, _, s_sum, acc = lax.fori_loop(0, nk, kv_step, init, unroll=True)
        o = acc / s_sum
        a = o[:tq] - lam_full * o[tq:]
        r = _rms(a, g_ref[...]) * (1.0 - lambda_init)
        o_ref[0, pl.ds(pl.multiple_of(qi * tq, tq), tq), :] = r.astype(o_ref.dtype)
        return carry

    lax.fori_loop(0, l // tq, q_block, 0)


def diff_attention_core(qkv, lam, subln_g, lambda_init, tq, kc):
    b, l, n3 = qkv.shape
    d = n3 // 3
    hd = 2 * DA_HEAD_DIM
    nh = d // hd
    tq, kc = min(tq, l), min(kc, l)
    assert l % tq == 0 and l % kc == 0
    return pl.pallas_call(
        functools.partial(_da_kernel, tq=tq, kc=kc, lambda_init=lambda_init),
        grid=(b, nh),
        in_specs=[
            pl.BlockSpec(lam.shape, lambda bi, h: (0, 0)),
            pl.BlockSpec((1, hd), lambda bi, h: (0, 0)),
            pl.BlockSpec((1, l, hd), lambda bi, h: (bi, 0, h)),
            pl.BlockSpec((1, l, hd), lambda bi, h: (bi, 0, nh + h)),
            pl.BlockSpec((1, l, hd), lambda bi, h: (bi, 0, 2 * nh + h)),
        ],
        out_specs=pl.BlockSpec((1, l, hd), lambda bi, h: (bi, 0, h)),
        out_shape=jax.ShapeDtypeStruct((b, l, d), BF16),
        compiler_params=_cparams(("parallel", "parallel")),
        name="diff_attn",
    )(lam, subln_g.reshape(1, hd), qkv, qkv, qkv)


def _na_kernel(q_ref, k_ref, v_ref, t2_ref, o_ref, *, rows, pairs, unroll):
    kr = NA_WIN_ROWS
    kw = kr * GRID_W
    scale = NA_HEAD_DIM ** -0.5
    rr = lax.broadcasted_iota(jnp.int32, (2 * GRID_W, 2 * GRID_W), 0) % GRID_W
    cc = lax.broadcasted_iota(jnp.int32, (2 * GRID_W, 2 * GRID_W), 1) % GRID_W
    cs = jnp.clip(rr - NA_WIN_COLS // 2, 0, GRID_W - NA_WIN_COLS)
    valid2 = (cc >= cs) & (cc < cs + NA_WIN_COLS)
    valid = jnp.concatenate([valid2] * (kr // 2), axis=1)
    lane = lax.broadcasted_iota(jnp.int32, (GRID_W, 2 * NA_HEAD_DIM), 1)

    def row_step(it, carry):
        jobs = []
        for u in range(unroll):
            r = it * unroll + u
            rs = jnp.clip(r - kr // 2, 0, rows - kr)
            for p in range(pairs):
                jobs.append((pl.ds(pl.multiple_of(r * GRID_W, GRID_W), GRID_W),
                             pl.ds(pl.multiple_of(rs * GRID_W, GRID_W), kw),
                             slice(p * 128, (p + 1) * 128), p, rs - r + NA_WIN_ROWS - 1))
        s = [_dot_nt(_block_diag_rows(q_ref[0, qs, cols]), k_ref[0, bs, cols]) * scale
             for qs, bs, cols, _, _ in jobs]
        bias = [jnp.concatenate([t2_ref[p, rho0 + 2 * j] for j in range(kr // 2)], axis=1)
                for _, _, _, p, rho0 in jobs]
        s = [jnp.where(valid, x + y, -1e30) for x, y in zip(s, bias)]
        m = [jnp.max(x, axis=-1, keepdims=True) for x in s]
        e = [jnp.exp(x - y) for x, y in zip(s, m)]
        den = [jnp.sum(x, axis=-1, keepdims=True) for x in e]
        o = [_dot(x.astype(BF16), v_ref[0, bs, cols]) / y for x, y, (_, bs, cols, _, _) in zip(e, den, jobs)]
        for x, (qs, _, cols, _, _) in zip(o, jobs):
            o_ref[0, qs, cols] = jnp.where(lane < NA_HEAD_DIM, x[:GRID_W], x[GRID_W:]).astype(o_ref.dtype)
        return carry

    lax.fori_loop(0, rows // unroll, row_step, 0)


def na_bias_table(rpb):
    nh = rpb.shape[0]
    cols = jnp.arange(GRID_W)
    cidx = jnp.clip(cols[None, :] - cols[:, None], -(NA_WIN_COLS - 1), NA_WIN_COLS - 1) + NA_WIN_COLS - 1
    t = rpb[:, :, cidx]
    t = t.reshape(nh // 2, 2, 2 * NA_WIN_ROWS - 1, GRID_W, GRID_W).transpose(0, 2, 1, 3, 4)
    t = t.reshape(nh // 2, 2 * NA_WIN_ROWS - 1, 2 * GRID_W, GRID_W)
    return jnp.concatenate([t[:, :-1], t[:, 1:]], axis=-1).astype(F32)


def neighborhood_attention_core(qkv, rpb):
    b, l, n3 = qkv.shape
    d = n3 // 3
    rows = l // GRID_W
    assert rows >= NA_WIN_ROWS and l % GRID_W == 0
    pairs = 2
    gw = pairs * 2 * NA_HEAD_DIM
    ng = d // gw
    t2 = na_bias_table(rpb)
    nrho = t2.shape[1]
    return pl.pallas_call(
        functools.partial(_na_kernel, rows=rows, pairs=pairs, unroll=2 if rows % 2 == 0 else 1),
        grid=(b, ng),
        in_specs=[
            pl.BlockSpec((1, l, gw), lambda bi, g: (bi, 0, g)),
            pl.BlockSpec((1, l, gw), lambda bi, g: (bi, 0, ng + g)),
            pl.BlockSpec((1, l, gw), lambda bi, g: (bi, 0, 2 * ng + g)),
            pl.BlockSpec((pairs, nrho, 2 * GRID_W, 2 * GRID_W), lambda bi, g: (g, 0, 0, 0)),
        ],
        out_specs=pl.BlockSpec((1, l, gw), lambda bi, g: (bi, 0, g)),
        out_shape=jax.ShapeDtypeStruct((b, l, d), BF16),
        compiler_params=_cparams(("parallel", "parallel")),
        name="nbr_attn",
    )(qkv, qkv, qkv, t2)


def s5_operators(a_re, a_im, log_dt, b_re, b_im, c_re, c_im):
    t = S5_CHUNK
    lam = lax.complex(a_re.astype(F32), a_im.astype(F32))
    dt = jnp.exp(log_dt.astype(F32))[..., None]
    lam_bar = jnp.exp(lam * dt)
    b_bar = ((lam_bar - 1.0) / lam)[..., None] * lax.complex(b_re.astype(F32), b_im.astype(F32))
    c_mat = lax.complex(c_re.astype(F32), c_im.astype(F32))
    taus = jnp.arange(t + 1, dtype=F32)
    pw = jnp.exp((lam * dt)[:, :, None, :] * taus[None, None, :, None])
    kern = jnp.real(jnp.einsum('dgpn,dgtn,dgnq->dgtpq', c_mat, pw[:, :, :t], b_bar, precision=HI))
    ti = jnp.arange(t)
    tau_f = ti[None, :] - ti[:, None]
    kf = jnp.where((tau_f >= 0)[None, :, :, None, None], kern[0][:, jnp.clip(tau_f, 0, t - 1)], 0.0)
    kb = jnp.where((tau_f <= 0)[None, :, :, None, None], kern[1][:, jnp.clip(-tau_f, 0, t - 1)], 0.0)
    g, p = kern.shape[1], kern.shape[3]
    wy_u = (kf + kb).transpose(0, 1, 4, 2, 3).reshape(g, t * p, t * p)

    def state_to_y(dr, powers):
        m = c_mat[dr][:, None, :, :] * pw[dr][:, powers, None, :]
        w = jnp.concatenate([jnp.real(m), -jnp.imag(m)], axis=-1)
        return w.transpose(0, 3, 1, 2).reshape(g, -1, t * p)

    wy_x = jnp.concatenate([state_to_y(0, ti + 1), state_to_y(1, t - ti)], axis=1)

    def u_to_state(dr, powers):
        m = pw[dr][:, powers, :, None] * b_bar[dr][:, None, :, :]
        w = jnp.concatenate([jnp.real(m), jnp.imag(m)], axis=2)
        return w.transpose(0, 1, 3, 2).reshape(g, t * p, -1)

    wb = jnp.concatenate([u_to_state(0, t - 1 - ti), u_to_state(1, ti)], axis=-1)
    a16 = pw[:, :, t, :]
    rot = jnp.stack([jnp.concatenate([jnp.real(a16), jnp.real(a16)], -1),
                     jnp.concatenate([-jnp.imag(a16), jnp.imag(a16)], -1)], axis=2)
    rot = rot.transpose(1, 0, 2, 3).reshape(g, 4, -1)
    return wb.astype(BF16), wy_u.astype(BF16), wy_x.astype(BF16), rot.astype(F32)


def _s5_kernel(u_ref, wb_ref, wyu_ref, wyx_ref, rot_ref, y_ref, s_ref, x_ref, *, nc, nb, gb):
    ns2 = 2 * S5_STATE
    for g in range(gb):
        s_ref[g] = _dot(u_ref[g], wb_ref[g])

    def step(c, xs):
        cb = nc - 1 - c
        rf = pl.ds(pl.multiple_of(c * nb, nb), nb)
        rb = pl.ds(pl.multiple_of(cb * nb, nb), nb)
        new = []
        for g in range(gb):
            xf, xb = xs[2 * g], xs[2 * g + 1]
            rot = rot_ref[g]
            x_ref[g, rf, 0:ns2] = xf.astype(BF16)
            x_ref[g, rb, ns2:2 * ns2] = xb.astype(BF16)
            xf = rot[0:1] * xf + rot[1:2] * pltpu.roll(xf, S5_STATE, 1) + s_ref[g, rf, 0:ns2]
            xb = rot[2:3] * xb + rot[3:4] * pltpu.roll(xb, S5_STATE, 1) + s_ref[g, rb, ns2:2 * ns2]
            new += [xf, xb]
        return tuple(new)

    lax.fori_loop(0, nc, step, tuple(jnp.zeros((nb, ns2), F32) for _ in range(2 * gb)))
    for g in range(gb):
        y_ref[g] = (_dot(u_ref[g], wyu_ref[g]) + _dot(x_ref[g], wyx_ref[g])).astype(y_ref.dtype)


def s5_core(u, ops, gb=2):
    b, l, d = u.shape
    t, p = S5_CHUNK, S5_GROUP
    g = d // p
    nc = l // t
    w = t * p
    wb, wy_u, wy_x, rot = ops
    ur = u.astype(BF16).reshape(b, nc, t, g, p).transpose(3, 1, 0, 2, 4).reshape(g, nc * b, w)
    grp = lambda shape: pl.BlockSpec((gb,) + shape, lambda i: (i, 0, 0))
    yr = pl.pallas_call(
        functools.partial(_s5_kernel, nc=nc, nb=b, gb=gb),
        grid=(g // gb,),
        in_specs=[grp((nc * b, w)), grp(wb.shape[1:]), grp(wy_u.shape[1:]), grp(wy_x.shape[1:]), grp(rot.shape[1:])],
        out_specs=grp((nc * b, w)),
        out_shape=jax.ShapeDtypeStruct((g, nc * b, w), BF16),
        scratch_shapes=[pltpu.VMEM((gb, nc * b, 4 * S5_STATE), F32), pltpu.VMEM((gb, nc * b, 4 * S5_STATE), BF16)],
        compiler_params=_cparams(("parallel",)),
        name="s5_scan",
    )(ur, wb, wy_u, wy_x, rot)
    return yr.reshape(g, nc, b, t, p).transpose(2, 1, 3, 0, 4).reshape(b, l, d)


def _split2(a):
    hi = a.astype(BF16)
    return hi, (a - hi.astype(F32)).astype(BF16)


def _dot3(a, b):
    (ah, al), (bh, bl) = a, b
    return _dot(ah, bh) + (_dot(ah, bl) + _dot(al, bh))


def _tri_inverse_minus_eye(ms):
    n = [-m for m in ms]
    sa = [_split2(x) for x in n]
    for _ in range(5):
        a = [_dot3(x, x) for x in sa]
        sa = [_split2(x) for x in a]
        n = [y + x + _dot3(_split2(y), sx) for x, y, sx in zip(a, n, sa)]
    return n


def _dn_kernel(hp_ref, gn_ref, wq_ref, wk_ref, wv_ref, q_ref, k_ref, v_ref, z_ref, ab_ref, o_ref,
               u_ref, wq2_ref, kd_ref, at_ref, cd_ref, of_ref, ob_ref, *, nheads, unroll):
    l = q_ref.shape[1]
    c = DN_CHUNK
    c2 = 2 * c
    n = l // c
    dk = DN_HEAD_DIM
    h = pl.program_id(1)
    lane_hp = lax.broadcasted_iota(jnp.int32, hp_ref.shape, 1)
    hp = jnp.sum(jnp.where(lane_hp == h, hp_ref[...], 0.0), axis=-1, keepdims=True)
    row = lax.broadcasted_iota(jnp.int32, (c2, c2), 0)
    col = lax.broadcasted_iota(jnp.int32, (c2, c2), 1)
    fwd = row < c
    same = fwd == (col < c)
    incl = same & ((fwd & (row >= col)) | (jnp.logical_not(fwd) & (row <= col)))
    strict = incl & (row != col)
    tri16 = incl.astype(F32).astype(BF16)
    fwd_col = fwd[:, 0:1]
    a_log = jnp.where(fwd_col, hp[0:1], hp[1:2])
    dt_b = jnp.where(fwd_col, hp[2:3], hp[3:4])
    lane_ab = lax.broadcasted_iota(jnp.int32, (c, LANES), 1)

    def conv_silu(x_ref, w_ref_, ci):
        base = pl.multiple_of(ci * c, c)
        x = x_ref[0, pl.ds(base, c), :]
        prev = x_ref[0, pl.ds(jnp.maximum(base - 8, 0), 8), :] * jnp.where(ci > 0, 1.0, 0.0)
        nxt = x_ref[0, pl.ds(jnp.minimum(base + c, l - 8), 8), :] * jnp.where(ci < n - 1, 1.0, 0.0)
        e = jnp.concatenate([prev, x, nxt], axis=0)
        ne = c + 16
        w = w_ref_[...]
        y = (w[0:1] * pltpu.roll(e, 1, 0) + w[1:2] * e
             + w[2:3] * pltpu.roll(e, ne - 1, 0) + w[3:4] * pltpu.roll(e, ne - 2, 0))[8:8 + c]
        return y * jax.nn.sigmoid(y)

    def pick(ci, j):
        ab = ab_ref[0, pl.ds(pl.multiple_of(ci * c, c), c), :]
        return jnp.sum(jnp.where(lane_ab == j * nheads + h, ab, 0.0), axis=-1, keepdims=True)

    lane = lax.broadcasted_iota(jnp.int32, (c2, dk), 1)

    def prep(it, carry):
        cis = [it * unroll + j for j in range(unroll)]
        two = lambda x: jnp.concatenate([x, x], axis=0)
        q = [conv_silu(q_ref, wq_ref, ci) for ci in cis]
        k = [conv_silu(k_ref, wk_ref, ci) for ci in cis]
        v = [two(conv_silu(v_ref, wv_ref, ci)) for ci in cis]
        q = [two(x * lax.rsqrt(jnp.sum(x * x, axis=-1, keepdims=True) + EPS) * (dk ** -0.5)) for x in q]
        k = [two(x * lax.rsqrt(jnp.sum(x * x, axis=-1, keepdims=True) + EPS)) for x in k]
        g_in = [jnp.concatenate([pick(ci, 0), pick(ci, 1)], axis=0) for ci in cis]
        beta = [jax.nn.sigmoid(jnp.concatenate([pick(ci, 2), pick(ci, 3)], axis=0)) for ci in cis]
        g = [-jnp.exp(a_log) * jax.nn.softplus(x + dt_b) for x in g_in]
        g_hi = [x.astype(BF16).astype(F32) for x in g]
        r1 = [x - y for x, y in zip(g, g_hi)]
        g_mid = [x.astype(BF16).astype(F32) for x in r1]
        g3 = [jnp.where(lane == 0, a_, jnp.where(lane == 1, b_, jnp.where(lane == 2, r_ - b_, 0.0)))
              for a_, b_, r_ in zip(g_hi, g_mid, r1)]
        gc = [jnp.broadcast_to(jnp.sum(_dot(tri16, x.astype(BF16)), axis=-1, keepdims=True), (c2, dk)) for x in g3]
        decay = [jnp.where(incl, jnp.exp(jnp.where(incl, x - x.T, 0.0)), 0.0) for x in gc]
        k16 = [x.astype(BF16) for x in k]
        kbeta = [x * y for x, y in zip(k, beta)]
        m = [jnp.where(strict, _dot_nt(x.astype(BF16), y) * z, 0.0) for x, y, z in zip(kbeta, k16, decay)]
        egc = [jnp.exp(x) for x in gc]
        rhs = [jnp.concatenate([x * y, z * w_], axis=1) for x, y, z, w_ in zip(v, beta, kbeta, egc)]
        ninv = _tri_inverse_minus_eye(m)
        sol = [x + _dot(y.astype(BF16), x.astype(BF16)) for x, y in zip(rhs, ninv)]
        attn = [(_dot_nt(x.astype(BF16), y) * z).astype(BF16) for x, y, z in zip(q, k16, decay)]
        for j, ci in enumerate(cis):
            gcj = gc[j]
            gc_last = jnp.where(fwd, gcj[c - 1:c], gcj[c:c + 1])
            qd = (q[j] * egc[j]).astype(BF16)
            w16 = sol[j][:, dk:].astype(BF16)
            kd = (k[j] * jnp.exp(gc_last - gcj)).astype(BF16)
            r1_ = pl.ds(pl.multiple_of(ci * c, c), c)
            r2_ = pl.ds(pl.multiple_of(ci * c2, c2), c2)
            r8_ = pl.ds(pl.multiple_of(ci * 8, 8), 8)
            for dr, half in enumerate((slice(0, c), slice(c, c2))):
                u_ref[dr, r1_, :] = sol[j][half, :dk]
                wq2_ref[dr, r2_, :] = jnp.concatenate([w16[half], qd[half]], axis=0)
                kd_ref[dr, r1_, :] = kd[half]
                at_ref[dr, r1_, :] = attn[j][half]
            cd_ref[0, r8_, :] = jnp.broadcast_to(jnp.exp(gcj[c - 1:c]), (8, dk))
            cd_ref[1, r8_, :] = jnp.broadcast_to(jnp.exp(gcj[c:c + 1]), (8, dk))
        return carry

    lax.fori_loop(0, n // unroll, prep, 0)

    def step(i, st):
        sf, sb = st
        ib = n - 1 - i
        rf = pl.ds(pl.multiple_of(i * c, c), c)
        rb = pl.ds(pl.multiple_of(ib * c, c), c)
        sf16, sb16 = sf.astype(BF16), sb.astype(BF16)
        pf = _dot(wq2_ref[0, pl.ds(pl.multiple_of(i * c2, c2), c2), :], sf16)
        pb = _dot(wq2_ref[1, pl.ds(pl.multiple_of(ib * c2, c2), c2), :], sb16)
        v_new = jnp.concatenate([u_ref[0, rf, :] - pf[:c], u_ref[1, rb, :] - pb[:c]], axis=0)
        v16 = v_new.astype(BF16)
        attn = jnp.concatenate([at_ref[0, rf, :], at_ref[1, rb, :]], axis=0)
        o = _dot(attn, v16) + jnp.concatenate([pf[c:], pb[c:]], axis=0)
        of_ref[rf, :] = o[:c]
        ob_ref[rb, :] = o[c:]
        cdf = cd_ref[0, pl.ds(pl.multiple_of(i * 8, 8), 1), :]
        cdb = cd_ref[1, pl.ds(pl.multiple_of(ib * 8, 8), 1), :]
        sf = sf * cdf + _dot_tn(kd_ref[0, rf, :], v16[:c])
        sb = sb * cdb + _dot_tn(kd_ref[1, rb, :], v16[c:])
        return sf, sb

    zero = jnp.zeros((dk, dk), F32)
    lax.fori_loop(0, n, step, (zero, zero))

    def finish(ci, carry):
        rows = pl.ds(pl.multiple_of(ci * c, c), c)
        z = z_ref[0, rows, :]
        r = _rms(of_ref[rows, :] + ob_ref[rows, :], gn_ref[...]) * (z * jax.nn.sigmoid(z))
        o_ref[0, rows, :] = r.astype(o_ref.dtype)
        return carry

    lax.fori_loop(0, n, finish, 0)


def gated_deltanet_core(proj, conv_w, a_log, dt_bias, onorm_g):
    b, l, _ = proj.shape
    d = conv_w.shape[1] // 3
    dk = DN_HEAD_DIM
    nh = d // dk
    n = l // DN_CHUNK
    assert l % DN_CHUNK == 0 and dk == LANES == 2 * DN_CHUNK
    unroll = 8 if n % 8 == 0 else 1
    hp =jnp.concatenate([a_log, dt_bias], axis=0).astype(F32)
    tok = lambda off: pl.BlockSpec((1, l, dk), lambda bi, h: (bi, 0, off + h))
    cw = lambda off: pl.BlockSpec((DN_CONV, dk), lambda bi, h: (0, off + h))
    return pl.pallas_call(
        functools.partial(_dn_kernel, nheads=nh, unroll=unroll),
        grid=(b, nh),
        in_specs=[
            pl.BlockSpec(hp.shape, lambda bi, h: (0, 0)),
            pl.BlockSpec((1, dk), lambda bi, h: (0, 0)),
            cw(0), cw(nh), cw(2 * nh),
            tok(0), tok(nh), tok(2 * nh), tok(3 * nh),
            pl.BlockSpec((1, l, dk), lambda bi, h: (bi, 0, 4 * nh)),
        ],
        out_specs=pl.BlockSpec((1, l, dk), lambda bi, h: (bi, 0, h)),
        out_shape=jax.ShapeDtypeStruct((b, l, d), BF16),
        scratch_shapes=[
            pltpu.VMEM((2, l, dk), F32), pltpu.VMEM((2, 2 * l, dk), BF16), pltpu.VMEM((2, l, dk), BF16),
            pltpu.VMEM((2, l, 2 * DN_CHUNK), BF16), pltpu.VMEM((2, n * 8, dk), F32),
            pltpu.VMEM((l, dk), F32), pltpu.VMEM((l, dk), F32),
        ],
        compiler_params=_cparams(("parallel", "parallel")),
        name="gated_deltanet",
    )(hp, onorm_g.reshape(1, dk), conv_w, conv_w, conv_w, proj, proj, proj, proj, proj)


def _rope_tables(l, width):
    half = DA_HEAD_DIM // 2
    inv = jnp.power(ROPE_THETA, -jnp.arange(half, dtype=F32) * 2.0 / DA_HEAD_DIM)
    ang = jnp.arange(l, dtype=F32)[:, None] * inv[None, :]
    cos, sin = jnp.cos(ang), jnp.sin(ang)
    reps = width // DA_HEAD_DIM
    return (jnp.tile(jnp.concatenate([cos, cos], axis=-1), (1, reps)),
            jnp.tile(jnp.concatenate([-sin, sin], axis=-1), (1, reps)))


def _pad_cols(w, mult):
    pad = -w.shape[1] % mult
    return jnp.pad(w, ((0, 0), (0, pad))) if pad else w


def _layer(x, mod_i, p, i, final):
    b, l, d = x.shape
    tm = min(1024, l)
    tn = 512
    tf = 256
    m, j = i % N_MIXERS, i // N_MIXERS
    sh1, sc1, g1, sh2, sc2, g2 = [mod_i[:, None, k * d:(k + 1) * d] for k in range(6)]
    n1 = p['norm1_g'][i]
    mode = "linear"
    if m == 0:
        lambda_init = 0.8 - 0.6 * math.exp(-0.3 * i)
        qkv = nm_matmul(x, n1, sc1, sh1, p['da_w_in'][j].astype(BF16), BF16, tm, tn,
                        rope=_rope_tables(l, tn) + (2 * d,))
        mix = diff_attention_core(qkv, p['da_lam'][j].astype(F32), p['da_subln_g'][j], lambda_init, 128, 512)
        wout = p['da_w_out'][j]
    elif m == 1:
        u = nm_matmul(x, n1, sc1, sh1, p['s5_w_in'][j].astype(BF16), F32, tm, tn)
        ops = s5_operators(p['s5_a_re'][j], p['s5_a_im'][j], p['s5_log_dt'][j], p['s5_b_re'][j],
                           p['s5_b_im'][j], p['s5_c_re'][j], p['s5_c_im'][j])
        mix = (s5_core(u, ops), u, p['s5_d'][j])
        wout = p['s5_w_glu'][j]
        mode = "glu"
    elif m == 2:
        qkv = nm_matmul(x, n1, sc1, sh1, p['na_w_in'][j].astype(BF16), BF16, tm, tn)
        mix = neighborhood_attention_core(qkv, p['na_rpb'][j])
        wout = p['na_w_out'][j]
    else:
        proj = nm_matmul(x, n1, sc1, sh1, _pad_cols(p['dn_w_in'][j], tn).astype(BF16), F32, tm, tn)
        mix = gated_deltanet_core(proj, p['dn_conv_w'][j], p['dn_a_log'][j], p['dn_dt_bias'][j],
                                  p['dn_onorm_g'][j])
        wout = p['dn_w_out'][j]
    return out_ffn(x, mix, wout.astype(BF16), g1, p['norm2_g'][i], sc2, sh2, g2,
                   p['ffn_w1'][i].astype(BF16), p['ffn_w3'][i].astype(BF16), p['ffn_w2'][i].astype(BF16),
                   p['final_g'] if final else None, mode, tm, tf)


def _encoder_trunk(x, mod, p):
    depth = p['ffn_w1'].shape[0]
    for i in range(depth):
        x = _layer(x, mod[i], p, i, i == depth - 1)
    return x


def kernel(x_prompt, x_sample, c_prompt, c_sample, ada_w, ada_b, norm1_g, norm2_g, ffn_w1, ffn_w3, ffn_w2, da_w_in, da_lam, da_subln_g, da_w_out, s5_w_in, s5_a_re, s5_a_im, s5_log_dt, s5_b_re, s5_b_im, s5_c_re, s5_c_im, s5_d, s5_w_glu, na_w_in, na_rpb, na_w_out, dn_w_in, dn_conv_w, dn_a_log, dn_dt_bias, dn_onorm_g, dn_w_out, final_g):
    p = dict(norm1_g=norm1_g, norm2_g=norm2_g, ffn_w1=ffn_w1, ffn_w3=ffn_w3, ffn_w2=ffn_w2,
             da_w_in=da_w_in, da_lam=da_lam, da_subln_g=da_subln_g, da_w_out=da_w_out,
             s5_w_in=s5_w_in, s5_a_re=s5_a_re, s5_a_im=s5_a_im, s5_log_dt=s5_log_dt,
             s5_b_re=s5_b_re, s5_b_im=s5_b_im, s5_c_re=s5_c_re, s5_c_im=s5_c_im, s5_d=s5_d, s5_w_glu=s5_w_glu,
             na_w_in=na_w_in, na_rpb=na_rpb, na_w_out=na_w_out,
             dn_w_in=dn_w_in, dn_conv_w=dn_conv_w, dn_a_log=dn_a_log, dn_dt_bias=dn_dt_bias,
             dn_onorm_g=dn_onorm_g, dn_w_out=dn_w_out, final_g=final_g)
    nb = x_prompt.shape[0]
    mod = adaln(jnp.concatenate([c_prompt, c_sample], axis=0), ada_w, ada_b)
    y_prompt = _encoder_trunk(x_prompt, mod[:, :nb], p)
    y_sample = _encoder_trunk(x_sample, mod[:, nb:], p)
    return (y_prompt, y_sample)
```

```python
import functools
import math

import jax
import jax.numpy as jnp
from jax import lax
from jax.experimental import pallas as pl
from jax.experimental.pallas import tpu as pltpu

F32 = jnp.float32
BF16 = jnp.bfloat16
HI = lax.Precision.HIGHEST

EPS = 1e-6
ROPE_THETA = 10000.0
N_MIXERS = 4
GRID_W = 64
DA_HEAD_DIM = 64
Q_BLOCK = 128
S5_GROUP = 16
S5_STATE = 64
S5_CHUNK = 16
NA_HEAD_DIM = 64
NA_WIN_ROWS = 8
NA_WIN_COLS = 16
DN_HEAD_DIM = 128
DN_CONV = 4
DN_CHUNK = 64

V7X_VMEM_LIMIT = 52 * 1024 * 1024
LANES = 128
MM_SUB_ROWS = 512
MM_SUB_COLS = 256


def _cparams(sem):
    return pltpu.CompilerParams(dimension_semantics=sem, vmem_limit_bytes=V7X_VMEM_LIMIT)


def _rms(xf, g):
    return xf * lax.rsqrt(jnp.mean(xf * xf, axis=-1, keepdims=True) + EPS) * g


def _dot(a, b):
    return jnp.dot(a, b, preferred_element_type=F32)


def _dot_nt(a, b):
    return lax.dot_general(a, b, (((1,), (1,)), ((), ())), preferred_element_type=F32)


def _dot_tn(a, b):
    return lax.dot_general(a, b, (((0,), (0,)), ((), ())), preferred_element_type=F32)


def _adaln_kernel(c_ref, w_ref, b_ref, o_ref):
    c = c_ref[...]
    a = c * jax.nn.sigmoid(c)
    o_ref[0] = jnp.dot(a, w_ref[0], preferred_element_type=F32, precision=HI) + b_ref[0]


def adaln(c, ada_w, ada_b):
    depth, d, n = ada_w.shape
    rows = c.shape[0]
    tn = 1536 if n % 1536 == 0 else n
    return pl.pallas_call(
        _adaln_kernel,
        grid=(depth, n // tn),
        in_specs=[
            pl.BlockSpec((rows, d), lambda i, j: (0, 0)),
            pl.BlockSpec((1, d, tn), lambda i, j: (i, 0, j)),
            pl.BlockSpec((1, 1, tn), lambda i, j: (i, 0, j)),
        ],
        out_specs=pl.BlockSpec((1, rows, tn), lambda i, j: (i, 0, j)),
        out_shape=jax.ShapeDtypeStruct((depth, rows, n), F32),
        compiler_params=_cparams(("parallel", "parallel")),
        name="adaln",
    )(c, ada_w, ada_b.reshape(depth, 1, n))


def _swap_halves(a, half):
    n = a.shape[-1]
    lane = lax.broadcasted_iota(jnp.int32, a.shape, a.ndim - 1)
    first = (lane % (2 * half)) < half
    return jnp.where(first, pltpu.roll(a, n - half, a.ndim - 1), pltpu.roll(a, half, a.ndim - 1))


def _nm_matmul_kernel(x_ref, g_ref, sc_ref, sh_ref, w_ref, *rest, rope_tiles):
    if rope_tiles:
        cos_ref, sin_ref, o_ref, h_ref = rest
    else:
        o_ref, h_ref = rest
    j = pl.program_id(2)

    @pl.when(j == 0)
    def _():
        h = _rms(x_ref[0], g_ref[...]) * (1.0 + sc_ref[0]) + sh_ref[0]
        h_ref[...] = h.astype(BF16)

    tm, tn = o_ref.shape[1], o_ref.shape[2]
    sm, sn = min(tm, MM_SUB_ROWS), min(tn, MM_SUB_COLS)

    def tiles(with_rope):
        for r0 in range(0, tm, sm):
            for c0 in range(0, tn, sn):
                rs, cs = slice(r0, r0 + sm), slice(c0, c0 + sn)
                acc = _dot(h_ref[rs, :], w_ref[:, cs])
                if with_rope:
                    acc = acc * cos_ref[rs, cs] + _swap_halves(acc, DA_HEAD_DIM // 2) * sin_ref[rs, cs]
                o_ref[0, rs, cs] = acc.astype(o_ref.dtype)

    if rope_tiles:
        pl.when(j < rope_tiles)(lambda: tiles(True))
        pl.when(j >= rope_tiles)(lambda: tiles(False))
    else:
        tiles(False)


def nm_matmul(x, g, sc, sh, w, out_dtype, tm, tn, rope=None):
    b, l, d = x.shape
    n = w.shape[1]
    assert l % tm == 0 and n % tn == 0
    in_specs = [
        pl.BlockSpec((1, tm, d), lambda bi, i, j: (bi, i, 0)),
        pl.BlockSpec((1, d), lambda bi, i, j: (0, 0)),
        pl.BlockSpec((1, 1, d), lambda bi, i, j: (bi, 0, 0)),
        pl.BlockSpec((1, 1, d), lambda bi, i, j: (bi, 0, 0)),
        pl.BlockSpec((d, tn), lambda bi, i, j: (0, j)),
    ]
    args = [x, g.reshape(1, d), sc, sh, w]
    rope_tiles = 0
    if rope is not None:
        cos_t, sin_t, rope_cols = rope
        rope_tiles = rope_cols // tn
        in_specs += [pl.BlockSpec((tm, tn), lambda bi, i, j: (i, 0))] * 2
        args += [cos_t, sin_t]
    return pl.pallas_call(
        functools.partial(_nm_matmul_kernel, rope_tiles=rope_tiles),
        grid=(b, l // tm, n // tn),
        in_specs=in_specs,
        out_specs=pl.BlockSpec((1, tm, tn), lambda bi, i, j: (bi, i, j)),
        out_shape=jax.ShapeDtypeStruct((b, l, n), out_dtype),
        scratch_shapes=[pltpu.VMEM((tm, d), BF16)],
        compiler_params=_cparams(("parallel", "parallel", "arbitrary")),
        name="nm_matmul",
    )(*args)


def _out_ffn_kernel(*refs, mode, final):
    it = iter(refs)
    x_ref = next(it)
    if mode == "glu":
        ys_ref, u_ref, dsk_ref = next(it), next(it), next(it)
    else:
        o_ref = next(it)
    wout_ref, g1_ref, gn_ref, sc_ref, sh_ref, g2_ref = (next(it) for _ in range(6))
    w1_ref, w3_ref, w2_ref = next(it), next(it), next(it)
    fg_ref = next(it) if final else None
    out_ref, x1_ref, h_ref, acc_ref = next(it), next(it), next(it), next(it)
    f = pl.program_id(2)
    d = x_ref.shape[-1]

    tm = x_ref.shape[1]
    sm, sn = min(tm, MM_SUB_ROWS), min(d, MM_SUB_COLS)
    row_blocks = [slice(r0, r0 + sm) for r0 in range(0, tm, sm)]

    @pl.when(f == 0)
    def _():
        for rs in row_blocks:
            if mode == "glu":
                o = jax.nn.gelu(ys_ref[0, rs, :] + dsk_ref[...] * u_ref[0, rs, :]).astype(BF16)
            else:
                o = o_ref[0, rs, :]
            for c0 in range(0, d, sn):
                cs = slice(c0, c0 + sn)
                y = _dot(o, wout_ref[:, cs])
                if mode == "glu":
                    y = y * jax.nn.sigmoid(_dot(o, wout_ref[:, slice(d + c0, d + c0 + sn)]))
                x1_ref[rs, cs] = x_ref[0, rs, cs] + g1_ref[0, :, cs] * y
            h = _rms(x1_ref[rs, :], gn_ref[...]) * (1.0 + sc_ref[0]) + sh_ref[0]
            h_ref[rs, :] = h.astype(BF16)
        acc_ref[...] = jnp.zeros_like(acc_ref)

    for rs in row_blocks:
        h = h_ref[rs, :]
        a = _dot(h, w1_ref[...])
        b = _dot(h, w3_ref[...])
        t = (a * jax.nn.sigmoid(a) * b).astype(BF16)
        acc_ref[rs, :] += _dot(t, w2_ref[...])

    @pl.when(f == pl.num_programs(2) - 1)
    def _():
        r = x1_ref[...] + g2_ref[0] * acc_ref[...]
        if final:
            r = _rms(r, fg_ref[...])
        out_ref[0] = r


def out_ffn(x, mix_in, wout, g1, gn, sc2, sh2, g2, w1, w3, w2, final_g, mode, tm, tf):
    b, l, d = x.shape
    fh = w1.shape[1]
    assert l % tm == 0 and fh % tf == 0
    tok = pl.BlockSpec((1, tm, d), lambda bi, i, f: (bi, i, 0))
    per_b = pl.BlockSpec((1, 1, d), lambda bi, i, f: (bi, 0, 0))
    vec = pl.BlockSpec((1, d), lambda bi, i, f: (0, 0))
    in_specs = [tok]
    args = [x]
    if mode == "glu":
        ys, u, dsk = mix_in
        in_specs += [tok, tok, vec]
        args += [ys, u, dsk.reshape(1, d)]
    else:
        in_specs += [tok]
        args += [mix_in]
    in_specs += [pl.BlockSpec(wout.shape, lambda bi, i, f: (0, 0), pipeline_mode=pl.Buffered(1)),
                 per_b, vec, per_b, per_b, per_b,
                 pl.BlockSpec((d, tf), lambda bi, i, f: (0, f)),
                 pl.BlockSpec((d, tf), lambda bi, i, f: (0, f)),
                 pl.BlockSpec((tf, d), lambda bi, i, f: (f, 0))]
    args += [wout, g1, gn.reshape(1, d), sc2, sh2, g2, w1, w3, w2]
    final = final_g is not None
    if final:
        in_specs += [vec]
        args += [final_g.reshape(1, d)]
    return pl.pallas_call(
        functools.partial(_out_ffn_kernel, mode=mode, final=final),
        grid=(b, l // tm, fh // tf),
        in_specs=in_specs,
        out_specs=tok,
        out_shape=jax.ShapeDtypeStruct((b, l, d), F32),
        scratch_shapes=[pltpu.VMEM((tm, d), F32), pltpu.VMEM((tm, d), BF16), pltpu.VMEM((tm, d), F32)],
        compiler_params=_cparams(("parallel", "parallel", "arbitrary")),
        name="out_ffn",
    )(*args)


def _block_diag_rows(q):
    lane = lax.broadcasted_iota(jnp.int32, q.shape, 1)
    zero = jnp.zeros_like(q)
    return jnp.concatenate([jnp.where(lane < 64, q, zero), jnp.where(lane >= 64, q, zero)], axis=0)


def _da_kernel(lam_ref, g_ref, q_ref, k_ref, v_ref, o_ref, *, tq, kc, lambda_init):
    l = q_ref.shape[1]
    lam = lam_ref[...]
    lam_full = (jnp.exp(jnp.sum(lam[0:1] * lam[1:2], axis=-1, keepdims=True))
                - jnp.exp(jnp.sum(lam[2:3] * lam[3:4], axis=-1, keepdims=True)) + lambda_init)
    scale = DA_HEAD_DIM ** -0.5

    def q_block(qi, carry):
        q = q_ref[0, pl.ds(pl.multiple_of(qi * tq, tq), tq), :]
        qbd = _block_diag_rows(q * scale)

        nk = l // kc
        scores = lambda ki: _dot_nt(qbd, k_ref[0, pl.ds(pl.multiple_of(ki * kc, kc), kc), :])

        def kv_step(ki, st):
            s, m, s_sum, acc = st
            s_next = scores(jnp.minimum(ki + 1, nk - 1))
            m_new = jnp.maximum(m, jnp.max(s, axis=-1, keepdims=True))
            alpha = jnp.exp(m - m_new)
            e = jnp.exp(s - m_new)
            s_sum = alpha * s_sum + jnp.sum(e, axis=-1, keepdims=True)
            acc = alpha * acc + _dot(e.astype(BF16), v_ref[0, pl.ds(pl.multiple_of(ki * kc, kc), kc), :])
            return s_next, m_new, s_sum, acc

        init = (scores(0), jnp.full((2 * tq, 1), -1e30, F32), jnp.zeros((2 * tq, 1), F32),
                jnp.zeros((2 * tq, 2 * DA_HEAD_DIM), F32))
        _, _, s_sum, acc = lax.fori_loop(0, nk, kv_step, init, unroll=True)
        o = acc / s_sum
        a = o[:tq] - lam_full * o[tq:]
        r = _rms(a, g_ref[...]) * (1.0 - lambda_init)
        o_ref[0, pl.ds(pl.multiple_of(qi * tq, tq), tq), :] = r.astype(o_ref.dtype)
        return carry

    lax.fori_loop(0, l // tq, q_block, 0, unroll=2)


def diff_attention_core(qkv, lam, subln_g, lambda_init, tq, kc):
    b, l, n3 = qkv.shape
    d = n3 // 3
    hd = 2 * DA_HEAD_DIM
    nh = d // hd
    tq, kc = min(tq, l), min(kc, l)
    assert l % tq == 0 and l % kc == 0
    return pl.pallas_call(
        functools.partial(_da_kernel, tq=tq, kc=kc, lambda_init=lambda_init),
        grid=(b, nh),
        in_specs=[
            pl.BlockSpec(lam.shape, lambda bi, h: (0, 0)),
            pl.BlockSpec((1, hd), lambda bi, h: (0, 0)),
            pl.BlockSpec((1, l, hd), lambda bi, h: (bi, 0, h)),
            pl.BlockSpec((1, l, hd), lambda bi, h: (bi, 0, nh + h)),
            pl.BlockSpec((1, l, hd), lambda bi, h: (bi, 0, 2 * nh + h)),
        ],
        out_specs=pl.BlockSpec((1, l, hd), lambda bi, h: (bi, 0, h)),
        out_shape=jax.ShapeDtypeStruct((b, l, d), BF16),
        compiler_params=_cparams(("parallel", "parallel")),
        name="diff_attn",
    )(lam, subln_g.reshape(1, hd), qkv, qkv, qkv)


def _na_kernel(q_ref, k_ref, v_ref, t2_ref, o_ref, *, rows, pairs, unroll):
    kr = NA_WIN_ROWS
    kw = kr * GRID_W
    scale = NA_HEAD_DIM ** -0.5
    rr = lax.broadcasted_iota(jnp.int32, (2 * GRID_W, 2 * GRID_W), 0) % GRID_W
    cc = lax.broadcasted_iota(jnp.int32, (2 * GRID_W, 2 * GRID_W), 1) % GRID_W
    cs = jnp.clip(rr - NA_WIN_COLS // 2, 0, GRID_W - NA_WIN_COLS)
    valid2 = (cc >= cs) & (cc < cs + NA_WIN_COLS)
    valid = jnp.concatenate([valid2] * (kr // 2), axis=1)
    lane = lax.broadcasted_iota(jnp.int32, (GRID_W, 2 * NA_HEAD_DIM), 1)

    def row_step(it, carry):
        jobs = []
        for u in range(unroll):
            r = it * unroll + u
            rs = jnp.clip(r - kr // 2, 0, rows - kr)
            for p in range(pairs):
                jobs.append((pl.ds(pl.multiple_of(r * GRID_W, GRID_W), GRID_W),
                             pl.ds(pl.multiple_of(rs * GRID_W, GRID_W), kw),
                             slice(p * 128, (p + 1) * 128), p, rs - r + NA_WIN_ROWS - 1))
        s = [_dot_nt(_block_diag_rows(q_ref[0, qs, cols]), k_ref[0, bs, cols]) * scale
             for qs, bs, cols, _, _ in jobs]
        bias = [jnp.concatenate([t2_ref[p, rho0 + 2 * j] for j in range(kr // 2)], axis=1)
                for _, _, _, p, rho0 in jobs]
        s = [jnp.where(valid, x + y, -1e30) for x, y in zip(s, bias)]
        m = [jnp.max(x, axis=-1, keepdims=True) for x in s]
        e = [jnp.exp(x - y) for x, y in zip(s, m)]
        den = [jnp.sum(x, axis=-1, keepdims=True) for x in e]
        o = [_dot(x.astype(BF16), v_ref[0, bs, cols]) / y for x, y, (_, bs, cols, _, _) in zip(e, den, jobs)]
        for x, (qs, _, cols, _, _) in zip(o, jobs):
            o_ref[0, qs, cols] = jnp.where(lane < NA_HEAD_DIM, x[:GRID_W], x[GRID_W:]).astype(o_ref.dtype)
        return carry

    lax.fori_loop(0, rows // unroll, row_step, 0)


def na_bias_table(rpb):
    nh = rpb.shape[0]
    cols = jnp.arange(GRID_W)
    cidx = jnp.clip(cols[None, :] - cols[:, None], -(NA_WIN_COLS - 1), NA_WIN_COLS - 1) + NA_WIN_COLS - 1
    t = rpb[:, :, cidx]
    t = t.reshape(nh // 2, 2, 2 * NA_WIN_ROWS - 1, GRID_W, GRID_W).transpose(0, 2, 1, 3, 4)
    t = t.reshape(nh // 2, 2 * NA_WIN_ROWS - 1, 2 * GRID_W, GRID_W)
    return jnp.concatenate([t[:, :-1], t[:, 1:]], axis=-1).astype(F32)


def neighborhood_attention_core(qkv, rpb):
    b, l, n3 = qkv.shape
    d = n3 // 3
    rows = l // GRID_W
    assert rows >= NA_WIN_ROWS and l % GRID_W == 0
    pairs = 2
    gw = pairs * 2 * NA_HEAD_DIM
    ng = d // gw
    t2 = na_bias_table(rpb)
    nrho = t2.shape[1]
    return pl.pallas_call(
        functools.partial(_na_kernel, rows=rows, pairs=pairs, unroll=2 if rows % 2 == 0 else 1),
        grid=(b, ng),
        in_specs=[
            pl.BlockSpec((1, l, gw), lambda bi, g: (bi, 0, g)),
            pl.BlockSpec((1, l, gw), lambda bi, g: (bi, 0, ng + g)),
            pl.BlockSpec((1, l, gw), lambda bi, g: (bi, 0, 2 * ng + g)),
            pl.BlockSpec((pairs, nrho, 2 * GRID_W, 2 * GRID_W), lambda bi, g: (g, 0, 0, 0)),
        ],
        out_specs=pl.BlockSpec((1, l, gw), lambda bi, g: (bi, 0, g)),
        out_shape=jax.ShapeDtypeStruct((b, l, d), BF16),
        compiler_params=_cparams(("parallel", "parallel")),
        name="nbr_attn",
    )(qkv, qkv, qkv, t2)


def s5_operators(a_re, a_im, log_dt, b_re, b_im, c_re, c_im):
    t = S5_CHUNK
    lam = lax.complex(a_re.astype(F32), a_im.astype(F32))
    dt = jnp.exp(log_dt.astype(F32))[..., None]
    lam_bar = jnp.exp(lam * dt)
    b_bar = ((lam_bar - 1.0) / lam)[..., None] * lax.complex(b_re.astype(F32), b_im.astype(F32))
    c_mat = lax.complex(c_re.astype(F32), c_im.astype(F32))
    taus = jnp.arange(t + 1, dtype=F32)
    pw = jnp.exp((lam * dt)[:, :, None, :] * taus[None, None, :, None])
    kern = jnp.real(jnp.einsum('dgpn,dgtn,dgnq->dgtpq', c_mat, pw[:, :, :t], b_bar, precision=HI))
    ti = jnp.arange(t)
    tau_f = ti[None, :] - ti[:, None]
    kf = jnp.where((tau_f >= 0)[None, :, :, None, None], kern[0][:, jnp.clip(tau_f, 0, t - 1)], 0.0)
    kb = jnp.where((tau_f <= 0)[None, :, :, None, None], kern[1][:, jnp.clip(-tau_f, 0, t - 1)], 0.0)
    g, p = kern.shape[1], kern.shape[3]
    wy_u = (kf + kb).transpose(0, 1, 4, 2, 3).reshape(g, t * p, t * p)

    def state_to_y(dr, powers):
        m = c_mat[dr][:, None, :, :] * pw[dr][:, powers, None, :]
        w = jnp.concatenate([jnp.real(m), -jnp.imag(m)], axis=-1)
        return w.transpose(0, 3, 1, 2).reshape(g, -1, t * p)

    wy_x = jnp.concatenate([state_to_y(0, ti + 1), state_to_y(1, t - ti)], axis=1)

    def u_to_state(dr, powers):
        m = pw[dr][:, powers, :, None] * b_bar[dr][:, None, :, :]
        w = jnp.concatenate([jnp.real(m), jnp.imag(m)], axis=2)
        return w.transpose(0, 1, 3, 2).reshape(g, t * p, -1)

    wb = jnp.concatenate([u_to_state(0, t - 1 - ti), u_to_state(1, ti)], axis=-1)
    a16 = pw[:, :, t, :]
    rot = jnp.stack([jnp.concatenate([jnp.real(a16), jnp.real(a16)], -1),
                     jnp.concatenate([-jnp.imag(a16), jnp.imag(a16)], -1)], axis=2)
    rot = rot.transpose(1, 0, 2, 3).reshape(g, 4, -1)
    return wb.astype(BF16), wy_u.astype(BF16), wy_x.astype(BF16), rot.astype(F32)


def _s5_kernel(u_ref, wb_ref, wyu_ref, wyx_ref, rot_ref, y_ref, s_ref, x_ref, *, nc, nb, gb):
    ns2 = 2 * S5_STATE
    for g in range(gb):
        s_ref[g] = _dot(u_ref[g], wb_ref[g])

    def step(c, xs):
        cb = nc - 1 - c
        rf = pl.ds(pl.multiple_of(c * nb, nb), nb)
        rb = pl.ds(pl.multiple_of(cb * nb, nb), nb)
        new = []
        for g in range(gb):
            xf, xb = xs[2 * g], xs[2 * g + 1]
            rot = rot_ref[g]
            x_ref[g, rf, 0:ns2] = xf.astype(BF16)
            x_ref[g, rb, ns2:2 * ns2] = xb.astype(BF16)
            xf = rot[0:1] * xf + rot[1:2] * pltpu.roll(xf, S5_STATE, 1) + s_ref[g, rf, 0:ns2]
            xb = rot[2:3] * xb + rot[3:4] * pltpu.roll(xb, S5_STATE, 1) + s_ref[g, rb, ns2:2 * ns2]
            new += [xf, xb]
        return tuple(new)

    lax.fori_loop(0, nc, step, tuple(jnp.zeros((nb, ns2), F32) for _ in range(2 * gb)))
    for g in range(gb):
        y_ref[g] = (_dot(u_ref[g], wyu_ref[g]) + _dot(x_ref[g], wyx_ref[g])).astype(y_ref.dtype)


def _s5_pack_kernel(u_ref, perm_ref, o_ref):
    w = o_ref.shape[2]
    for gl in range(o_ref.shape[0]):
        o_ref[gl] = _dot(u_ref[0], perm_ref[:, gl * w:(gl + 1) * w]).astype(o_ref.dtype)


def _s5_unpack_kernel(y_ref, perm_t_ref, o_ref):
    ycat = jnp.concatenate([y_ref[gl] for gl in range(y_ref.shape[0])], axis=1)
    w = y_ref.shape[2]
    for c0 in range(0, o_ref.shape[2], w):
        o_ref[0, :, c0:c0 + w] = _dot(ycat, perm_t_ref[:, c0:c0 + w]).astype(o_ref.dtype)


def _s5_relayout(x, perm, pack):
    gl = LANES // S5_GROUP
    w = S5_CHUNK * S5_GROUP
    rows = x.shape[1]
    nj = x.shape[0] if pack else x.shape[0] // gl
    tr = min(MM_SUB_ROWS, rows)
    assert rows % tr == 0
    wide = pl.BlockSpec((1, tr, gl * w), lambda j, i: (j, i, 0))
    narrow = pl.BlockSpec((gl, tr, w), lambda j, i: (j, i, 0))
    return pl.pallas_call(
        _s5_pack_kernel if pack else _s5_unpack_kernel,
        grid=(nj, rows // tr),
        in_specs=[wide if pack else narrow,
                  pl.BlockSpec(perm.shape, lambda j, i: (0, 0), pipeline_mode=pl.Buffered(1))],
        out_specs=narrow if pack else wide,
        out_shape=jax.ShapeDtypeStruct((nj * gl, rows, w) if pack else (nj, rows, gl * w), BF16),
        compiler_params=_cparams(("parallel", "parallel")),
        name="s5_pack" if pack else "s5_unpack",
    )(x, perm)


def _s5_lane_perm():
    t, gl, p = S5_CHUNK, LANES // S5_GROUP, S5_GROUP
    src = jnp.arange(t * gl * p).reshape(t, gl, p).transpose(1, 0, 2).reshape(-1)
    return (jnp.arange(t * gl * p)[:, None] == src[None, :]).astype(BF16)


def s5_core(u, ops, gb=2):
    b, l, d = u.shape
    t, p = S5_CHUNK, S5_GROUP
    g = d // p
    nc = l // t
    w = t * p
    nj = d // LANES
    wb, wy_u, wy_x, rot = ops
    perm = _s5_lane_perm()
    uj = u.astype(BF16).reshape(b, nc, t, nj, LANES).transpose(3, 1, 0, 2, 4).reshape(nj, nc * b, t * LANES)
    ur = _s5_relayout(uj, perm, pack=True)
    grp = lambda shape: pl.BlockSpec((gb,) + shape, lambda i: (i, 0, 0))
    yr = pl.pallas_call(
        functools.partial(_s5_kernel, nc=nc, nb=b, gb=gb),
        grid=(g // gb,),
        in_specs=[grp((nc * b, w)), grp(wb.shape[1:]), grp(wy_u.shape[1:]), grp(wy_x.shape[1:]), grp(rot.shape[1:])],
        out_specs=grp((nc * b, w)),
        out_shape=jax.ShapeDtypeStruct((g, nc * b, w), BF16),
        scratch_shapes=[pltpu.VMEM((gb, nc * b, 4 * S5_STATE), F32), pltpu.VMEM((gb, nc * b, 4 * S5_STATE), BF16)],
        compiler_params=_cparams(("parallel",)),
        name="s5_scan",
    )(ur, wb, wy_u, wy_x, rot)
    yj = _s5_relayout(yr, perm.T, pack=False)
    return yj.reshape(nj, nc, b, t, LANES).transpose(2, 1, 3, 0, 4).reshape(b, l, d)


def _split2(a):
    hi = a.astype(BF16)
    return hi, (a - hi.astype(F32)).astype(BF16)


def _dot3(a, b):
    (ah, al), (bh, bl) = a, b
    return _dot(ah, bh) + (_dot(ah, bl) + _dot(al, bh))


def _tri_inverse_minus_eye(ms):
    n = [-m for m in ms]
    sa = [_split2(x) for x in n]
    for _ in range(5):
        a = [_dot3(x, x) for x in sa]
        sa = [_split2(x) for x in a]
        n = [y + x + _dot3(_split2(y), sx) for x, y, sx in zip(a, n, sa)]
    return n


def _dn_kernel(hp_ref, gn_ref, wq_ref, wk_ref, wv_ref, q_ref, k_ref, v_ref, z_ref, ab_ref, o_ref,
               u_ref, wq2_ref, kd_ref, at_ref, cd_ref, of_ref, ob_ref, *, nheads, unroll):
    l = q_ref.shape[1]
    c = DN_CHUNK
    c2 = 2 * c
    n = l // c
    dk = DN_HEAD_DIM
    h = pl.program_id(1)
    lane_hp = lax.broadcasted_iota(jnp.int32, hp_ref.shape, 1)
    hp = jnp.sum(jnp.where(lane_hp == h, hp_ref[...], 0.0), axis=-1, keepdims=True)
    row = lax.broadcasted_iota(jnp.int32, (c2, c2), 0)
    col = lax.broadcasted_iota(jnp.int32, (c2, c2), 1)
    fwd = row < c
    same = fwd == (col < c)
    incl = same & ((fwd & (row >= col)) | (jnp.logical_not(fwd) & (row <= col)))
    strict = incl & (row != col)
    tri16 = incl.astype(F32).astype(BF16)
    fwd_col = fwd[:, 0:1]
    a_log = jnp.where(fwd_col, hp[0:1], hp[1:2])
    dt_b = jnp.where(fwd_col, hp[2:3], hp[3:4])
    lane_ab = lax.broadcasted_iota(jnp.int32, (c, LANES), 1)

    def conv_silu(x_ref, w_ref_, ci):
        base = pl.multiple_of(ci * c, c)
        x = x_ref[0, pl.ds(base, c), :]
        prev = x_ref[0, pl.ds(jnp.maximum(base - 8, 0), 8), :] * jnp.where(ci > 0, 1.0, 0.0)
        nxt = x_ref[0, pl.ds(jnp.minimum(base + c, l - 8), 8), :] * jnp.where(ci < n - 1, 1.0, 0.0)
        e = jnp.concatenate([prev, x, nxt], axis=0)
        ne = c + 16
        w = w_ref_[...]
        y = (w[0:1] * pltpu.roll(e, 1, 0) + w[1:2] * e
             + w[2:3] * pltpu.roll(e, ne - 1, 0) + w[3:4] * pltpu.roll(e, ne - 2, 0))[8:8 + c]
        return y * jax.nn.sigmoid(y)

    def pick(ci, j):
        ab = ab_ref[0, pl.ds(pl.multiple_of(ci * c, c), c), :]
        return jnp.sum(jnp.where(lane_ab == j * nheads + h, ab, 0.0), axis=-1, keepdims=True)

    lane = lax.broadcasted_iota(jnp.int32, (c2, dk), 1)

    def prep(it, carry):
        cis = [it * unroll + j for j in range(unroll)]
        two = lambda x: jnp.concatenate([x, x], axis=0)
        q = [conv_silu(q_ref, wq_ref, ci) for ci in cis]
        k = [conv_silu(k_ref, wk_ref, ci) for ci in cis]
        v = [two(conv_silu(v_ref, wv_ref, ci)) for ci in cis]
        q = [two(x * lax.rsqrt(jnp.sum(x * x, axis=-1, keepdims=True) + EPS) * (dk ** -0.5)) for x in q]
        k = [two(x * lax.rsqrt(jnp.sum(x * x, axis=-1, keepdims=True) + EPS)) for x in k]
        g_in = [jnp.concatenate([pick(ci, 0), pick(ci, 1)], axis=0) for ci in cis]
        beta = [jax.nn.sigmoid(jnp.concatenate([pick(ci, 2), pick(ci, 3)], axis=0)) for ci in cis]
        g = [-jnp.exp(a_log) * jax.nn.softplus(x + dt_b) for x in g_in]
        g_hi = [x.astype(BF16).astype(F32) for x in g]
        r1 = [x - y for x, y in zip(g, g_hi)]
        g_mid = [x.astype(BF16).astype(F32) for x in r1]
        g3 = [jnp.where(lane == 0, a_, jnp.where(lane == 1, b_, jnp.where(lane == 2, r_ - b_, 0.0)))
              for a_, b_, r_ in zip(g_hi, g_mid, r1)]
        gc = [jnp.broadcast_to(jnp.sum(_dot(tri16, x.astype(BF16)), axis=-1, keepdims=True), (c2, dk)) for x in g3]
        decay = [jnp.where(incl, jnp.exp(jnp.where(incl, x - x.T, 0.0)), 0.0) for x in gc]
        k16 = [x.astype(BF16) for x in k]
        kbeta = [x * y for x, y in zip(k, beta)]
        m = [jnp.where(strict, _dot_nt(x.astype(BF16), y) * z, 0.0) for x, y, z in zip(kbeta, k16, decay)]
        egc = [jnp.exp(x) for x in gc]
        rhs = [jnp.concatenate([x * y, z * w_], axis=1) for x, y, z, w_ in zip(v, beta, kbeta, egc)]
        ninv = _tri_inverse_minus_eye(m)
        sol = [x + _dot(y.astype(BF16), x.astype(BF16)) for x, y in zip(rhs, ninv)]
        attn = [(_dot_nt(x.astype(BF16), y) * z).astype(BF16) for x, y, z in zip(q, k16, decay)]
        for j, ci in enumerate(cis):
            gcj = gc[j]
            gc_last = jnp.where(fwd, gcj[c - 1:c], gcj[c:c + 1])
            qd = (q[j] * egc[j]).astype(BF16)
            w16 = sol[j][:, dk:].astype(BF16)
            kd = (k[j] * jnp.exp(gc_last - gcj)).astype(BF16)
            r1_ = pl.ds(pl.multiple_of(ci * c, c), c)
            r2_ = pl.ds(pl.multiple_of(ci * c2, c2), c2)
            r8_ = pl.ds(pl.multiple_of(ci * 8, 8), 8)
            for dr, half in enumerate((slice(0, c), slice(c, c2))):
                u_ref[dr, r1_, :] = sol[j][half, :dk]
                wq2_ref[dr, r2_, :] = jnp.concatenate([w16[half], qd[half]], axis=0)
                kd_ref[dr, r1_, :] = kd[half]
                at_ref[dr, r1_, :] = attn[j][half]
            cd_ref[0, r8_, :] = jnp.broadcast_to(jnp.exp(gcj[c - 1:c]), (8, dk))
            cd_ref[1, r8_, :] = jnp.broadcast_to(jnp.exp(gcj[c:c + 1]), (8, dk))
        return carry

    lax.fori_loop(0, n // unroll, prep, 0)

    def step(i, st):
        sf, sb = st
        ib = n - 1 - i
        rf = pl.ds(pl.multiple_of(i * c, c), c)
        rb = pl.ds(pl.multiple_of(ib * c, c), c)
        sf16, sb16 = sf.astype(BF16), sb.astype(BF16)
        pf = _dot(wq2_ref[0, pl.ds(pl.multiple_of(i * c2, c2), c2), :], sf16)
        pb = _dot(wq2_ref[1, pl.ds(pl.multiple_of(ib * c2, c2), c2), :], sb16)
        v_new = jnp.concatenate([u_ref[0, rf, :] - pf[:c], u_ref[1, rb, :] - pb[:c]], axis=0)
        v16 = v_new.astype(BF16)
        attn = jnp.concatenate([at_ref[0, rf, :], at_ref[1, rb, :]], axis=0)
        o = _dot(attn, v16) + jnp.concatenate([pf[c:], pb[c:]], axis=0)
        of_ref[rf, :] = o[:c]
        ob_ref[rb, :] = o[c:]
        cdf = cd_ref[0, pl.ds(pl.multiple_of(i * 8, 8), 1), :]
        cdb = cd_ref[1, pl.ds(pl.multiple_of(ib * 8, 8), 1), :]
        sf = sf * cdf + _dot_tn(kd_ref[0, rf, :], v16[:c])
        sb = sb * cdb + _dot_tn(kd_ref[1, rb, :], v16[c:])
        return sf, sb

    zero = jnp.zeros((dk, dk), F32)
    lax.fori_loop(0, n, step, (zero, zero), unroll=2)

    def finish(ci, carry):
        rows = pl.ds(pl.multiple_of(ci * c, c), c)
        z = z_ref[0, rows, :]
        r = _rms(of_ref[rows, :] + ob_ref[rows, :], gn_ref[...]) * (z * jax.nn.sigmoid(z))
        o_ref[0, rows, :] = r.astype(o_ref.dtype)
        return carry

    lax.fori_loop(0, n, finish, 0, unroll=4)


def gated_deltanet_core(proj, conv_w, a_log, dt_bias, onorm_g):
    b, l, _ = proj.shape
    d = conv_w.shape[1] // 3
    dk = DN_HEAD_DIM
    nh = d // dk
    n = l // DN_CHUNK
    assert l % DN_CHUNK == 0 and dk == LANES == 2 * DN_CHUNK
    unroll = 8 if n % 8 == 0 else 1
    hp =jnp.concatenate([a_log, dt_bias], axis=0).astype(F32)
    tok = lambda off: pl.BlockSpec((1, l, dk), lambda bi, h: (bi, 0, off + h))
    cw = lambda off: pl.BlockSpec((DN_CONV, dk), lambda bi, h: (0, off + h))
    return pl.pallas_call(
        functools.partial(_dn_kernel, nheads=nh, unroll=unroll),
        grid=(b, nh),
        in_specs=[
            pl.BlockSpec(hp.shape, lambda bi, h: (0, 0)),
            pl.BlockSpec((1, dk), lambda bi, h: (0, 0)),
            cw(0), cw(nh), cw(2 * nh),
            tok(0), tok(nh), tok(2 * nh), tok(3 * nh),
            pl.BlockSpec((1, l, dk), lambda bi, h: (bi, 0, 4 * nh)),
        ],
        out_specs=pl.BlockSpec((1, l, dk), lambda bi, h: (bi, 0, h)),
        out_shape=jax.ShapeDtypeStruct((b, l, d), BF16),
        scratch_shapes=[
            pltpu.VMEM((2, l, dk), F32), pltpu.VMEM((2, 2 * l, dk), BF16), pltpu.VMEM((2, l, dk), BF16),
            pltpu.VMEM((2, l, 2 * DN_CHUNK), BF16), pltpu.VMEM((2, n * 8, dk), F32),
            pltpu.VMEM((l, dk), F32), pltpu.VMEM((l, dk), F32),
        ],
        compiler_params=_cparams(("parallel", "parallel")),
        name="gated_deltanet",
    )(hp, onorm_g.reshape(1, dk), conv_w, conv_w, conv_w, proj, proj, proj, proj, proj)


def _rope_tables(l, width):
    half = DA_HEAD_DIM // 2
    inv = jnp.power(ROPE_THETA, -jnp.arange(half, dtype=F32) * 2.0 / DA_HEAD_DIM)
    ang = jnp.arange(l, dtype=F32)[:, None] * inv[None, :]
    cos, sin = jnp.cos(ang), jnp.sin(ang)
    reps = width // DA_HEAD_DIM
    return (jnp.tile(jnp.concatenate([cos, cos], axis=-1), (1, reps)),
            jnp.tile(jnp.concatenate([-sin, sin], axis=-1), (1, reps)))


def _pad_cols(w, mult):
    pad = -w.shape[1] % mult
    return jnp.pad(w, ((0, 0), (0, pad))) if pad else w


def _layer(x, mod_i, p, i, final):
    b, l, d = x.shape
    tm = min(1024, l)
    tn = 512
    tf = 256
    m, j = i % N_MIXERS, i // N_MIXERS
    sh1, sc1, g1, sh2, sc2, g2 = [mod_i[:, None, k * d:(k + 1) * d] for k in range(6)]
    n1 = p['norm1_g'][i]
    mode = "linear"
    if m == 0:
        lambda_init = 0.8 - 0.6 * math.exp(-0.3 * i)
        qkv = nm_matmul(x, n1, sc1, sh1, p['da_w_in'][j].astype(BF16), BF16, tm, tn,
                        rope=_rope_tables(l, tn) + (2 * d,))
        mix = diff_attention_core(qkv, p['da_lam'][j].astype(F32), p['da_subln_g'][j], lambda_init, 128, 512)
        wout = p['da_w_out'][j]
    elif m == 1:
        u = nm_matmul(x, n1, sc1, sh1, p['s5_w_in'][j].astype(BF16), F32, tm, tn)
        ops = s5_operators(p['s5_a_re'][j], p['s5_a_im'][j], p['s5_log_dt'][j], p['s5_b_re'][j],
                           p['s5_b_im'][j], p['s5_c_re'][j], p['s5_c_im'][j])
        mix = (s5_core(u, ops), u, p['s5_d'][j])
        wout = p['s5_w_glu'][j]
        mode = "glu"
    elif m == 2:
        qkv = nm_matmul(x, n1, sc1, sh1, p['na_w_in'][j].astype(BF16), BF16, tm, tn)
        mix = neighborhood_attention_core(qkv, p['na_rpb'][j])
        wout = p['na_w_out'][j]
    else:
        proj = nm_matmul(x, n1, sc1, sh1, _pad_cols(p['dn_w_in'][j], tn).astype(BF16), F32, tm, tn)
        mix = gated_deltanet_core(proj, p['dn_conv_w'][j], p['dn_a_log'][j], p['dn_dt_bias'][j],
                                  p['dn_onorm_g'][j])
        wout = p['dn_w_out'][j]
    return out_ffn(x, mix, wout.astype(BF16), g1, p['norm2_g'][i], sc2, sh2, g2,
                   p['ffn_w1'][i].astype(BF16), p['ffn_w3'][i].astype(BF16), p['ffn_w2'][i].astype(BF16),
                   p['final_g'] if final else None, mode, tm, tf)


def _encoder_trunk(x, mod, p):
    depth = p['ffn_w1'].shape[0]
    for i in range(depth):
        x = _layer(x, mod[i], p, i, i == depth - 1)
    return x


def kernel(x_prompt, x_sample, c_prompt, c_sample, ada_w, ada_b, norm1_g, norm2_g, ffn_w1, ffn_w3, ffn_w2, da_w_in, da_lam, da_subln_g, da_w_out, s5_w_in, s5_a_re, s5_a_im, s5_log_dt, s5_b_re, s5_b_im, s5_c_re, s5_c_im, s5_d, s5_w_glu, na_w_in, na_rpb, na_w_out, dn_w_in, dn_conv_w, dn_a_log, dn_dt_bias, dn_onorm_g, dn_w_out, final_g):
    p = dict(norm1_g=norm1_g, norm2_g=norm2_g, ffn_w1=ffn_w1, ffn_w3=ffn_w3, ffn_w2=ffn_w2,
             da_w_in=da_w_in, da_lam=da_lam, da_subln_g=da_subln_g, da_w_out=da_w_out,
             s5_w_in=s5_w_in, s5_a_re=s5_a_re, s5_a_im=s5_a_im, s5_log_dt=s5_log_dt,
             s5_b_re=s5_b_re, s5_b_im=s5_b_im, s5_c_re=s5_c_re, s5_c_im=s5_c_im, s5_d=s5_d, s5_w_glu=s5_w_glu,
             na_w_in=na_w_in, na_rpb=na_rpb, na_w_out=na_w_out,
             dn_w_in=dn_w_in, dn_conv_w=dn_conv_w, dn_a_log=dn_a_log, dn_dt_bias=dn_dt_bias,
             dn_onorm_g=dn_onorm_g, dn_w_out=dn_w_out, final_g=final_g)
    nb = x_prompt.shape[0]
    mod = adaln(jnp.concatenate([c_prompt, c_sample], axis=0), ada_w, ada_b)
    y_prompt = _encoder_trunk(x_prompt, mod[:, :nb], p)
    y_sample = _encoder_trunk(x_sample, mod[:, nb:], p)
    return (y_prompt, y_sample)
```

```python
import functools
import math

import jax
import jax.numpy as jnp
from jax import lax
from jax.experimental import pallas as pl
from jax.experimental.pallas import tpu as pltpu

F32 = jnp.float32
BF16 = jnp.bfloat16
HI = lax.Precision.HIGHEST

EPS = 1e-6
ROPE_THETA = 10000.0
N_MIXERS = 4
GRID_W = 64
DA_HEAD_DIM = 64
Q_BLOCK = 128
S5_GROUP = 16
S5_STATE = 64
S5_CHUNK = 16
NA_HEAD_DIM = 64
NA_WIN_ROWS = 8
NA_WIN_COLS = 16
DN_HEAD_DIM = 128
DN_CONV = 4
DN_CHUNK = 64

V7X_VMEM_LIMIT = 52 * 1024 * 1024
LANES = 128
MM_SUB_ROWS = 512
MM_SUB_COLS = 256


def _cparams(sem):
    return pltpu.CompilerParams(dimension_semantics=sem, vmem_limit_bytes=V7X_VMEM_LIMIT)


def _rms(xf, g):
    return xf * lax.rsqrt(jnp.mean(xf * xf, axis=-1, keepdims=True) + EPS) * g


def _dot(a, b):
    return jnp.dot(a, b, preferred_element_type=F32)


def _dot_nt(a, b):
    return lax.dot_general(a, b, (((1,), (1,)), ((), ())), preferred_element_type=F32)


def _dot_tn(a, b):
    return lax.dot_general(a, b, (((0,), (0,)), ((), ())), preferred_element_type=F32)


def _adaln_kernel(c_ref, w_ref, b_ref, o_ref):
    c = c_ref[...]
    a = c * jax.nn.sigmoid(c)
    o_ref[0] = jnp.dot(a, w_ref[0], preferred_element_type=F32, precision=HI) + b_ref[0]


def adaln(c, ada_w, ada_b):
    depth, d, n = ada_w.shape
    rows = c.shape[0]
    tn = 1536 if n % 1536 == 0 else n
    return pl.pallas_call(
        _adaln_kernel,
        grid=(depth, n // tn),
        in_specs=[
            pl.BlockSpec((rows, d), lambda i, j: (0, 0)),
            pl.BlockSpec((1, d, tn), lambda i, j: (i, 0, j)),
            pl.BlockSpec((1, 1, tn), lambda i, j: (i, 0, j)),
        ],
        out_specs=pl.BlockSpec((1, rows, tn), lambda i, j: (i, 0, j)),
        out_shape=jax.ShapeDtypeStruct((depth, rows, n), F32),
        compiler_params=_cparams(("parallel", "parallel")),
        name="adaln",
    )(c, ada_w, ada_b.reshape(depth, 1, n))


def _swap_halves(a, half):
    n = a.shape[-1]
    lane = lax.broadcasted_iota(jnp.int32, a.shape, a.ndim - 1)
    first = (lane % (2 * half)) < half
    return jnp.where(first, pltpu.roll(a, n - half, a.ndim - 1), pltpu.roll(a, half, a.ndim - 1))


def _nm_matmul_kernel(x_ref, g_ref, sc_ref, sh_ref, w_ref, *rest, rope_tiles):
    if rope_tiles:
        cos_ref, sin_ref, o_ref, h_ref = rest
    else:
        o_ref, h_ref = rest
    j = pl.program_id(2)

    @pl.when(j == 0)
    def _():
        h = _rms(x_ref[0], g_ref[...]) * (1.0 + sc_ref[0]) + sh_ref[0]
        h_ref[...] = h.astype(BF16)

    tm, tn = o_ref.shape[1], o_ref.shape[2]
    sm, sn = min(tm, MM_SUB_ROWS), min(tn, MM_SUB_COLS)

    def tiles(with_rope):
        for r0 in range(0, tm, sm):
            for c0 in range(0, tn, sn):
                rs, cs = slice(r0, r0 + sm), slice(c0, c0 + sn)
                acc = _dot(h_ref[rs, :], w_ref[:, cs])
                if with_rope:
                    acc = acc * cos_ref[rs, cs] + _swap_halves(acc, DA_HEAD_DIM // 2) * sin_ref[rs, cs]
                o_ref[0, rs, cs] = acc.astype(o_ref.dtype)

    if rope_tiles:
        pl.when(j < rope_tiles)(lambda: tiles(True))
        pl.when(j >= rope_tiles)(lambda: tiles(False))
    else:
        tiles(False)


def nm_matmul(x, g, sc, sh, w, out_dtype, tm, tn, rope=None):
    b, l, d = x.shape
    n = w.shape[1]
    assert l % tm == 0 and n % tn == 0
    in_specs = [
        pl.BlockSpec((1, tm, d), lambda bi, i, j: (bi, i, 0)),
        pl.BlockSpec((1, d), lambda bi, i, j: (0, 0)),
        pl.BlockSpec((1, 1, d), lambda bi, i, j: (bi, 0, 0)),
        pl.BlockSpec((1, 1, d), lambda bi, i, j: (bi, 0, 0)),
        pl.BlockSpec((d, tn), lambda bi, i, j: (0, j)),
    ]
    args = [x, g.reshape(1, d), sc, sh, w]
    rope_tiles = 0
    if rope is not None:
        cos_t, sin_t, rope_cols = rope
        rope_tiles = rope_cols // tn
        in_specs += [pl.BlockSpec((tm, tn), lambda bi, i, j: (i, 0))] * 2
        args += [cos_t, sin_t]
    return pl.pallas_call(
        functools.partial(_nm_matmul_kernel, rope_tiles=rope_tiles),
        grid=(b, l // tm, n // tn),
        in_specs=in_specs,
        out_specs=pl.BlockSpec((1, tm, tn), lambda bi, i, j: (bi, i, j)),
        out_shape=jax.ShapeDtypeStruct((b, l, n), out_dtype),
        scratch_shapes=[pltpu.VMEM((tm, d), BF16)],
        compiler_params=_cparams(("parallel", "parallel", "arbitrary")),
        name="nm_matmul",
    )(*args)


def _out_ffn_kernel(*refs, mode, final):
    it = iter(refs)
    x_ref = next(it)
    if mode == "glu":
        ys_ref, u_ref, dsk_ref = next(it), next(it), next(it)
    else:
        o_ref = next(it)
    wout_ref, g1_ref, gn_ref, sc_ref, sh_ref, g2_ref = (next(it) for _ in range(6))
    w1_ref, w3_ref, w2_ref = next(it), next(it), next(it)
    fg_ref = next(it) if final else None
    out_ref, x1_ref, h_ref, acc_ref = next(it), next(it), next(it), next(it)
    f = pl.program_id(2)
    d = x_ref.shape[-1]

    tm = x_ref.shape[1]
    sm, sn = min(tm, MM_SUB_ROWS), min(d, MM_SUB_COLS)
    row_blocks = [slice(r0, r0 + sm) for r0 in range(0, tm, sm)]

    @pl.when(f == 0)
    def _():
        for rs in row_blocks:
            if mode == "glu":
                o = jax.nn.gelu(ys_ref[0, rs, :] + dsk_ref[...] * u_ref[0, rs, :]).astype(BF16)
            else:
                o = o_ref[0, rs, :]
            for c0 in range(0, d, sn):
                cs = slice(c0, c0 + sn)
                y = _dot(o, wout_ref[:, cs])
                if mode == "glu":
                    y = y * jax.nn.sigmoid(_dot(o, wout_ref[:, slice(d + c0, d + c0 + sn)]))
                x1_ref[rs, cs] = x_ref[0, rs, cs] + g1_ref[0, :, cs] * y
            h = _rms(x1_ref[rs, :], gn_ref[...]) * (1.0 + sc_ref[0]) + sh_ref[0]
            h_ref[rs, :] = h.astype(BF16)
        acc_ref[...] = jnp.zeros_like(acc_ref)

    for rs in row_blocks:
        h = h_ref[rs, :]
        a = _dot(h, w1_ref[...])
        b = _dot(h, w3_ref[...])
        t = (a * jax.nn.sigmoid(a) * b).astype(BF16)
        acc_ref[rs, :] += _dot(t, w2_ref[...])

    @pl.when(f == pl.num_programs(2) - 1)
    def _():
        r = x1_ref[...] + g2_ref[0] * acc_ref[...]
        if final:
            r = _rms(r, fg_ref[...])
        out_ref[0] = r


def out_ffn(x, mix_in, wout, g1, gn, sc2, sh2, g2, w1, w3, w2, final_g, mode, tm, tf):
    b, l, d = x.shape
    fh = w1.shape[1]
    assert l % tm == 0 and fh % tf == 0
    tok = pl.BlockSpec((1, tm, d), lambda bi, i, f: (bi, i, 0))
    per_b = pl.BlockSpec((1, 1, d), lambda bi, i, f: (bi, 0, 0))
    vec = pl.BlockSpec((1, d), lambda bi, i, f: (0, 0))
    in_specs = [tok]
    args = [x]
    if mode == "glu":
        ys, u, dsk = mix_in
        in_specs += [tok, tok, vec]
        args += [ys, u, dsk.reshape(1, d)]
    else:
        in_specs += [tok]
        args += [mix_in]
    in_specs += [pl.BlockSpec(wout.shape, lambda bi, i, f: (0, 0), pipeline_mode=pl.Buffered(1)),
                 per_b, vec, per_b, per_b, per_b,
                 pl.BlockSpec((d, tf), lambda bi, i, f: (0, f)),
                 pl.BlockSpec((d, tf), lambda bi, i, f: (0, f)),
                 pl.BlockSpec((tf, d), lambda bi, i, f: (f, 0))]
    args += [wout, g1, gn.reshape(1, d), sc2, sh2, g2, w1, w3, w2]
    final = final_g is not None
    if final:
        in_specs += [vec]
        args += [final_g.reshape(1, d)]
    return pl.pallas_call(
        functools.partial(_out_ffn_kernel, mode=mode, final=final),
        grid=(b, l // tm, fh // tf),
        in_specs=in_specs,
        out_specs=tok,
        out_shape=jax.ShapeDtypeStruct((b, l, d), F32),
        scratch_shapes=[pltpu.VMEM((tm, d), F32), pltpu.VMEM((tm, d), BF16), pltpu.VMEM((tm, d), F32)],
        compiler_params=_cparams(("parallel", "parallel", "arbitrary")),
        name="out_ffn",
    )(*args)


def _block_diag_rows(q):
    lane = lax.broadcasted_iota(jnp.int32, q.shape, 1)
    zero = jnp.zeros_like(q)
    return jnp.concatenate([jnp.where(lane < 64, q, zero), jnp.where(lane >= 64, q, zero)], axis=0)


def _da_kernel(lam_ref, g_ref, q_ref, k_ref, v_ref, o_ref, *, tq, kc, lambda_init):
    l = q_ref.shape[1]
    lam = lam_ref[...]
    lam_full = (jnp.exp(jnp.sum(lam[0:1] * lam[1:2], axis=-1, keepdims=True))
                - jnp.exp(jnp.sum(lam[2:3] * lam[3:4], axis=-1, keepdims=True)) + lambda_init)
    scale = DA_HEAD_DIM ** -0.5

    def q_block(qi, carry):
        q = q_ref[0, pl.ds(pl.multiple_of(qi * tq, tq), tq), :]
        qbd = _block_diag_rows(q * scale)

        nk = l // kc
        scores = lambda ki: _dot_nt(qbd, k_ref[0, pl.ds(pl.multiple_of(ki * kc, kc), kc), :])

        def kv_step(ki, st):
            s, m, s_sum, acc = st
            s_next = scores(jnp.minimum(ki + 1, nk - 1))
            m_new = jnp.maximum(m, jnp.max(s, axis=-1, keepdims=True))
            alpha = jnp.exp(m - m_new)
            e = jnp.exp(s - m_new)
            s_sum = alpha * s_sum + jnp.sum(e, axis=-1, keepdims=True)
            acc = alpha * acc + _dot(e.astype(BF16), v_ref[0, pl.ds(pl.multiple_of(ki * kc, kc), kc), :])
            return s_next, m_new, s_sum, acc

        init = (scores(0), jnp.full((2 * tq, 1), -1e30, F32), jnp.zeros((2 * tq, 1), F32),
                jnp.zeros((2 * tq, 2 * DA_HEAD_DIM), F32))
        _, _, s_sum, acc = lax.fori_loop(0, nk, kv_step, init, unroll=True)
        o = acc / s_sum
        a = o[:tq] - lam_full * o[tq:]
        r = _rms(a, g_ref[...]) * (1.0 - lambda_init)
        o_ref[0, pl.ds(pl.multiple_of(qi * tq, tq), tq), :] = r.astype(o_ref.dtype)
        return carry

    lax.fori_loop(0, l // tq, q_block, 0, unroll=2)


def diff_attention_core(qkv, lam, subln_g, lambda_init, tq, kc):
    b, l, n3 = qkv.shape
    d = n3 // 3
    hd = 2 * DA_HEAD_DIM
    nh = d // hd
    tq, kc = min(tq, l), min(kc, l)
    assert l % tq == 0 and l % kc == 0
    return pl.pallas_call(
        functools.partial(_da_kernel, tq=tq, kc=kc, lambda_init=lambda_init),
        grid=(b, nh),
        in_specs=[
            pl.BlockSpec(lam.shape, lambda bi, h: (0, 0)),
            pl.BlockSpec((1, hd), lambda bi, h: (0, 0)),
            pl.BlockSpec((1, l, hd), lambda bi, h: (bi, 0, h)),
            pl.BlockSpec((1, l, hd), lambda bi, h: (bi, 0, nh + h)),
            pl.BlockSpec((1, l, hd), lambda bi, h: (bi, 0, 2 * nh + h)),
        ],
        out_specs=pl.BlockSpec((1, l, hd), lambda bi, h: (bi, 0, h)),
        out_shape=jax.ShapeDtypeStruct((b, l, d), BF16),
        compiler_params=_cparams(("parallel", "parallel")),
        name="diff_attn",
    )(lam, subln_g.reshape(1, hd), qkv, qkv, qkv)


def _na_kernel(q_ref, k_ref, v_ref, t2_ref, o_ref, *, rows, pairs, unroll):
    kr = NA_WIN_ROWS
    kw = kr * GRID_W
    scale = NA_HEAD_DIM ** -0.5
    rr = lax.broadcasted_iota(jnp.int32, (2 * GRID_W, 2 * GRID_W), 0) % GRID_W
    cc = lax.broadcasted_iota(jnp.int32, (2 * GRID_W, 2 * GRID_W), 1) % GRID_W
    cs = jnp.clip(rr - NA_WIN_COLS // 2, 0, GRID_W - NA_WIN_COLS)
    valid2 = (cc >= cs) & (cc < cs + NA_WIN_COLS)
    valid = jnp.concatenate([valid2] * (kr // 2), axis=1)
    lane = lax.broadcasted_iota(jnp.int32, (GRID_W, 2 * NA_HEAD_DIM), 1)

    def row_step(it, carry):
        jobs = []
        for u in range(unroll):
            r = it * unroll + u
            rs = jnp.clip(r - kr // 2, 0, rows - kr)
            for p in range(pairs):
                jobs.append((pl.ds(pl.multiple_of(r * GRID_W, GRID_W), GRID_W),
                             pl.ds(pl.multiple_of(rs * GRID_W, GRID_W), kw),
                             slice(p * 128, (p + 1) * 128), p, rs - r + NA_WIN_ROWS - 1))
        s = [_dot_nt(_block_diag_rows(q_ref[0, qs, cols]), k_ref[0, bs, cols]) * scale
             for qs, bs, cols, _, _ in jobs]
        bias = [jnp.concatenate([t2_ref[p, rho0 + 2 * j] for j in range(kr // 2)], axis=1)
                for _, _, _, p, rho0 in jobs]
        s = [jnp.where(valid, x + y, -1e30) for x, y in zip(s, bias)]
        m = [jnp.max(x, axis=-1, keepdims=True) for x in s]
        e = [jnp.exp(x - y) for x, y in zip(s, m)]
        den = [jnp.sum(x, axis=-1, keepdims=True) for x in e]
        o = [_dot(x.astype(BF16), v_ref[0, bs, cols]) / y for x, y, (_, bs, cols, _, _) in zip(e, den, jobs)]
        for x, (qs, _, cols, _, _) in zip(o, jobs):
            o_ref[0, qs, cols] = jnp.where(lane < NA_HEAD_DIM, x[:GRID_W], x[GRID_W:]).astype(o_ref.dtype)
        return carry

    lax.fori_loop(0, rows // unroll, row_step, 0)


def na_bias_table(rpb):
    nh = rpb.shape[0]
    cols = jnp.arange(GRID_W)
    cidx = jnp.clip(cols[None, :] - cols[:, None], -(NA_WIN_COLS - 1), NA_WIN_COLS - 1) + NA_WIN_COLS - 1
    t = rpb[:, :, cidx]
    t = t.reshape(nh // 2, 2, 2 * NA_WIN_ROWS - 1, GRID_W, GRID_W).transpose(0, 2, 1, 3, 4)
    t = t.reshape(nh // 2, 2 * NA_WIN_ROWS - 1, 2 * GRID_W, GRID_W)
    return jnp.concatenate([t[:, :-1], t[:, 1:]], axis=-1).astype(F32)


def neighborhood_attention_core(qkv, rpb):
    b, l, n3 = qkv.shape
    d = n3 // 3
    rows = l // GRID_W
    assert rows >= NA_WIN_ROWS and l % GRID_W == 0
    pairs = 2
    gw = pairs * 2 * NA_HEAD_DIM
    ng = d // gw
    t2 = na_bias_table(rpb)
    nrho = t2.shape[1]
    return pl.pallas_call(
        functools.partial(_na_kernel, rows=rows, pairs=pairs, unroll=4 if rows % 4 == 0 else 1),
        grid=(b, ng),
        in_specs=[
            pl.BlockSpec((1, l, gw), lambda bi, g: (bi, 0, g)),
            pl.BlockSpec((1, l, gw), lambda bi, g: (bi, 0, ng + g)),
            pl.BlockSpec((1, l, gw), lambda bi, g: (bi, 0, 2 * ng + g)),
            pl.BlockSpec((pairs, nrho, 2 * GRID_W, 2 * GRID_W), lambda bi, g: (g, 0, 0, 0)),
        ],
        out_specs=pl.BlockSpec((1, l, gw), lambda bi, g: (bi, 0, g)),
        out_shape=jax.ShapeDtypeStruct((b, l, d), BF16),
        compiler_params=_cparams(("parallel", "parallel")),
        name="nbr_attn",
    )(qkv, qkv, qkv, t2)


def s5_operators(a_re, a_im, log_dt, b_re, b_im, c_re, c_im):
    t = S5_CHUNK
    lam = lax.complex(a_re.astype(F32), a_im.astype(F32))
    dt = jnp.exp(log_dt.astype(F32))[..., None]
    lam_bar = jnp.exp(lam * dt)
    b_bar = ((lam_bar - 1.0) / lam)[..., None] * lax.complex(b_re.astype(F32), b_im.astype(F32))
    c_mat = lax.complex(c_re.astype(F32), c_im.astype(F32))
    taus = jnp.arange(t + 1, dtype=F32)
    pw = jnp.exp((lam * dt)[:, :, None, :] * taus[None, None, :, None])
    kern = jnp.real(jnp.einsum('dgpn,dgtn,dgnq->dgtpq', c_mat, pw[:, :, :t], b_bar, precision=HI))
    ti = jnp.arange(t)
    tau_f = ti[None, :] - ti[:, None]
    kf = jnp.where((tau_f >= 0)[None, :, :, None, None], kern[0][:, jnp.clip(tau_f, 0, t - 1)], 0.0)
    kb = jnp.where((tau_f <= 0)[None, :, :, None, None], kern[1][:, jnp.clip(-tau_f, 0, t - 1)], 0.0)
    g, p = kern.shape[1], kern.shape[3]
    wy_u = (kf + kb).transpose(0, 1, 4, 2, 3).reshape(g, t * p, t * p)

    def state_to_y(dr, powers):
        m = c_mat[dr][:, None, :, :] * pw[dr][:, powers, None, :]
        w = jnp.concatenate([jnp.real(m), -jnp.imag(m)], axis=-1)
        return w.transpose(0, 3, 1, 2).reshape(g, -1, t * p)

    wy_x = jnp.concatenate([state_to_y(0, ti + 1), state_to_y(1, t - ti)], axis=1)

    def u_to_state(dr, powers):
        m = pw[dr][:, powers, :, None] * b_bar[dr][:, None, :, :]
        w = jnp.concatenate([jnp.real(m), jnp.imag(m)], axis=2)
        return w.transpose(0, 1, 3, 2).reshape(g, t * p, -1)

    wb = jnp.concatenate([u_to_state(0, t - 1 - ti), u_to_state(1, ti)], axis=-1)
    a16 = pw[:, :, t, :]
    rot = jnp.stack([jnp.concatenate([jnp.real(a16), jnp.real(a16)], -1),
                     jnp.concatenate([-jnp.imag(a16), jnp.imag(a16)], -1)], axis=2)
    rot = rot.transpose(1, 0, 2, 3).reshape(g, 4, -1)
    return wb.astype(BF16), wy_u.astype(BF16), wy_x.astype(BF16), rot.astype(F32)


def _s5_kernel(u_ref, wb_ref, wyu_ref, wyx_ref, rot_ref, y_ref, s_ref, x_ref, *, nc, nb, gb):
    ns2 = 2 * S5_STATE
    for g in range(gb):
        s_ref[g] = _dot(u_ref[g], wb_ref[g])

    def step(c, xs):
        cb = nc - 1 - c
        rf = pl.ds(pl.multiple_of(c * nb, nb), nb)
        rb = pl.ds(pl.multiple_of(cb * nb, nb), nb)
        new = []
        for g in range(gb):
            xf, xb = xs[2 * g], xs[2 * g + 1]
            rot = rot_ref[g]
            x_ref[g, rf, 0:ns2] = xf.astype(BF16)
            x_ref[g, rb, ns2:2 * ns2] = xb.astype(BF16)
            xf = rot[0:1] * xf + rot[1:2] * pltpu.roll(xf, S5_STATE, 1) + s_ref[g, rf, 0:ns2]
            xb = rot[2:3] * xb + rot[3:4] * pltpu.roll(xb, S5_STATE, 1) + s_ref[g, rb, ns2:2 * ns2]
            new += [xf, xb]
        return tuple(new)

    lax.fori_loop(0, nc, step, tuple(jnp.zeros((nb, ns2), F32) for _ in range(2 * gb)))
    for g in range(gb):
        y_ref[g] = (_dot(u_ref[g], wyu_ref[g]) + _dot(x_ref[g], wyx_ref[g])).astype(y_ref.dtype)


def _s5_pack_kernel(u_ref, perm_ref, o_ref):
    w = o_ref.shape[2]
    for gl in range(o_ref.shape[0]):
        o_ref[gl] = _dot(u_ref[0], perm_ref[:, gl * w:(gl + 1) * w]).astype(o_ref.dtype)


def _s5_unpack_kernel(y_ref, perm_t_ref, o_ref):
    ycat = jnp.concatenate([y_ref[gl] for gl in range(y_ref.shape[0])], axis=1)
    w = y_ref.shape[2]
    for c0 in range(0, o_ref.shape[2], w):
        o_ref[0, :, c0:c0 + w] = _dot(ycat, perm_t_ref[:, c0:c0 + w]).astype(o_ref.dtype)


def _s5_relayout(x, perm, pack):
    gl = LANES // S5_GROUP
    w = S5_CHUNK * S5_GROUP
    rows = x.shape[1]
    nj = x.shape[0] if pack else x.shape[0] // gl
    tr = min(MM_SUB_ROWS, rows)
    assert rows % tr == 0
    wide = pl.BlockSpec((1, tr, gl * w), lambda j, i: (j, i, 0))
    narrow = pl.BlockSpec((gl, tr, w), lambda j, i: (j, i, 0))
    return pl.pallas_call(
        _s5_pack_kernel if pack else _s5_unpack_kernel,
        grid=(nj, rows // tr),
        in_specs=[wide if pack else narrow,
                  pl.BlockSpec(perm.shape, lambda j, i: (0, 0), pipeline_mode=pl.Buffered(1))],
        out_specs=narrow if pack else wide,
        out_shape=jax.ShapeDtypeStruct((nj * gl, rows, w) if pack else (nj, rows, gl * w), BF16),
        compiler_params=_cparams(("parallel", "parallel")),
        name="s5_pack" if pack else "s5_unpack",
    )(x, perm)


def _s5_lane_perm():
    t, gl, p = S5_CHUNK, LANES // S5_GROUP, S5_GROUP
    src = jnp.arange(t * gl * p).reshape(t, gl, p).transpose(1, 0, 2).reshape(-1)
    return (jnp.arange(t * gl * p)[:, None] == src[None, :]).astype(BF16)


def s5_core(u, ops, gb=2):
    b, l, d = u.shape
    t, p = S5_CHUNK, S5_GROUP
    g = d // p
    nc = l // t
    w = t * p
    nj = d // LANES
    wb, wy_u, wy_x, rot = ops
    perm = _s5_lane_perm()
    uj = u.astype(BF16).reshape(b, nc, t, nj, LANES).transpose(3, 1, 0, 2, 4).reshape(nj, nc * b, t * LANES)
    ur = _s5_relayout(uj, perm, pack=True)
    grp = lambda shape: pl.BlockSpec((gb,) + shape, lambda i: (i, 0, 0))
    yr = pl.pallas_call(
        functools.partial(_s5_kernel, nc=nc, nb=b, gb=gb),
        grid=(g // gb,),
        in_specs=[grp((nc * b, w)), grp(wb.shape[1:]), grp(wy_u.shape[1:]), grp(wy_x.shape[1:]), grp(rot.shape[1:])],
        out_specs=grp((nc * b, w)),
        out_shape=jax.ShapeDtypeStruct((g, nc * b, w), BF16),
        scratch_shapes=[pltpu.VMEM((gb, nc * b, 4 * S5_STATE), F32), pltpu.VMEM((gb, nc * b, 4 * S5_STATE), BF16)],
        compiler_params=_cparams(("parallel",)),
        name="s5_scan",
    )(ur, wb, wy_u, wy_x, rot)
    yj = _s5_relayout(yr, perm.T, pack=False)
    return yj.reshape(nj, nc, b, t, LANES).transpose(2, 1, 3, 0, 4).reshape(b, l, d)


def _split2(a):
    hi = a.astype(BF16)
    return hi, (a - hi.astype(F32)).astype(BF16)


def _dot3(a, b):
    (ah, al), (bh, bl) = a, b
    return _dot(ah, bh) + (_dot(ah, bl) + _dot(al, bh))


def _tri_inverse_minus_eye(ms):
    n = [-m for m in ms]
    sa = [_split2(x) for x in n]
    for _ in range(5):
        a = [_dot3(x, x) for x in sa]
        sa = [_split2(x) for x in a]
        n = [y + x + _dot3(_split2(y), sx) for x, y, sx in zip(a, n, sa)]
    return n


def _dn_kernel(hp_ref, gn_ref, wq_ref, wk_ref, wv_ref, q_ref, k_ref, v_ref, z_ref, ab_ref, o_ref,
               u_ref, wq2_ref, kd_ref, at_ref, cd_ref, of_ref, ob_ref, *, nheads, unroll):
    l = q_ref.shape[1]
    c = DN_CHUNK
    c2 = 2 * c
    n = l // c
    dk = DN_HEAD_DIM
    h = pl.program_id(1)
    lane_hp = lax.broadcasted_iota(jnp.int32, hp_ref.shape, 1)
    hp = jnp.sum(jnp.where(lane_hp == h, hp_ref[...], 0.0), axis=-1, keepdims=True)
    row = lax.broadcasted_iota(jnp.int32, (c2, c2), 0)
    col = lax.broadcasted_iota(jnp.int32, (c2, c2), 1)
    fwd = row < c
    same = fwd == (col < c)
    incl = same & ((fwd & (row >= col)) | (jnp.logical_not(fwd) & (row <= col)))
    strict = incl & (row != col)
    tri16 = incl.astype(F32).astype(BF16)
    fwd_col = fwd[:, 0:1]
    a_log = jnp.where(fwd_col, hp[0:1], hp[1:2])
    dt_b = jnp.where(fwd_col, hp[2:3], hp[3:4])
    lane_ab = lax.broadcasted_iota(jnp.int32, (c, LANES), 1)

    def conv_silu(x_ref, w_ref_, ci):
        base = pl.multiple_of(ci * c, c)
        x = x_ref[0, pl.ds(base, c), :]
        prev = x_ref[0, pl.ds(jnp.maximum(base - 8, 0), 8), :] * jnp.where(ci > 0, 1.0, 0.0)
        nxt = x_ref[0, pl.ds(jnp.minimum(base + c, l - 8), 8), :] * jnp.where(ci < n - 1, 1.0, 0.0)
        e = jnp.concatenate([prev, x, nxt], axis=0)
        ne = c + 16
        w = w_ref_[...]
        y = (w[0:1] * pltpu.roll(e, 1, 0) + w[1:2] * e
             + w[2:3] * pltpu.roll(e, ne - 1, 0) + w[3:4] * pltpu.roll(e, ne - 2, 0))[8:8 + c]
        return y * jax.nn.sigmoid(y)

    def pick(ci, j):
        ab = ab_ref[0, pl.ds(pl.multiple_of(ci * c, c), c), :]
        return jnp.sum(jnp.where(lane_ab == j * nheads + h, ab, 0.0), axis=-1, keepdims=True)

    lane = lax.broadcasted_iota(jnp.int32, (c2, dk), 1)

    per_end = unroll // 2

    def prep(it):
        cis = [it * per_end + j for j in range(per_end)] + [n - 1 - it * per_end - j for j in range(per_end)]
        two = lambda x: jnp.concatenate([x, x], axis=0)
        q = [conv_silu(q_ref, wq_ref, ci) for ci in cis]
        k = [conv_silu(k_ref, wk_ref, ci) for ci in cis]
        v = [two(conv_silu(v_ref, wv_ref, ci)) for ci in cis]
        q = [two(x * lax.rsqrt(jnp.sum(x * x, axis=-1, keepdims=True) + EPS) * (dk ** -0.5)) for x in q]
        k = [two(x * lax.rsqrt(jnp.sum(x * x, axis=-1, keepdims=True) + EPS)) for x in k]
        g_in = [jnp.concatenate([pick(ci, 0), pick(ci, 1)], axis=0) for ci in cis]
        beta = [jax.nn.sigmoid(jnp.concatenate([pick(ci, 2), pick(ci, 3)], axis=0)) for ci in cis]
        g = [-jnp.exp(a_log) * jax.nn.softplus(x + dt_b) for x in g_in]
        g_hi = [x.astype(BF16).astype(F32) for x in g]
        r1 = [x - y for x, y in zip(g, g_hi)]
        g_mid = [x.astype(BF16).astype(F32) for x in r1]
        g3 = [jnp.where(lane == 0, a_, jnp.where(lane == 1, b_, jnp.where(lane == 2, r_ - b_, 0.0)))
              for a_, b_, r_ in zip(g_hi, g_mid, r1)]
        gc = [jnp.broadcast_to(jnp.sum(_dot(tri16, x.astype(BF16)), axis=-1, keepdims=True), (c2, dk)) for x in g3]
        decay = [jnp.where(incl, jnp.exp(jnp.where(incl, x - x.T, 0.0)), 0.0) for x in gc]
        k16 = [x.astype(BF16) for x in k]
        kbeta = [x * y for x, y in zip(k, beta)]
        m = [jnp.where(strict, _dot_nt(x.astype(BF16), y) * z, 0.0) for x, y, z in zip(kbeta, k16, decay)]
        egc = [jnp.exp(x) for x in gc]
        rhs = [jnp.concatenate([x * y, z * w_], axis=1) for x, y, z, w_ in zip(v, beta, kbeta, egc)]
        ninv = _tri_inverse_minus_eye(m)
        sol = [x + _dot(y.astype(BF16), x.astype(BF16)) for x, y in zip(rhs, ninv)]
        attn = [(_dot_nt(x.astype(BF16), y) * z).astype(BF16) for x, y, z in zip(q, k16, decay)]
        for j, ci in enumerate(cis):
            gcj = gc[j]
            gc_last = jnp.where(fwd, gcj[c - 1:c], gcj[c:c + 1])
            qd = (q[j] * egc[j]).astype(BF16)
            w16 = sol[j][:, dk:].astype(BF16)
            kd = (k[j] * jnp.exp(gc_last - gcj)).astype(BF16)
            r1_ = pl.ds(pl.multiple_of(ci * c, c), c)
            r2_ = pl.ds(pl.multiple_of(ci * c2, c2), c2)
            r8_ = pl.ds(pl.multiple_of(ci * 8, 8), 8)
            for dr, half in enumerate((slice(0, c), slice(c, c2))):
                u_ref[dr, r1_, :] = sol[j][half, :dk]
                wq2_ref[dr, r2_, :] = jnp.concatenate([w16[half], qd[half]], axis=0)
                kd_ref[dr, r1_, :] = kd[half]
                at_ref[dr, r1_, :] = attn[j][half]
            cd_ref[0, r8_, :] = jnp.broadcast_to(jnp.exp(gcj[c - 1:c]), (8, dk))
            cd_ref[1, r8_, :] = jnp.broadcast_to(jnp.exp(gcj[c:c + 1]), (8, dk))

    def step(i, st):
        sf, sb = st
        ib = n - 1 - i
        rf = pl.ds(pl.multiple_of(i * c, c), c)
        rb = pl.ds(pl.multiple_of(ib * c, c), c)
        sf16, sb16 = sf.astype(BF16), sb.astype(BF16)
        pf = _dot(wq2_ref[0, pl.ds(pl.multiple_of(i * c2, c2), c2), :], sf16)
        pb = _dot(wq2_ref[1, pl.ds(pl.multiple_of(ib * c2, c2), c2), :], sb16)
        v_new = jnp.concatenate([u_ref[0, rf, :] - pf[:c], u_ref[1, rb, :] - pb[:c]], axis=0)
        v16 = v_new.astype(BF16)
        attn = jnp.concatenate([at_ref[0, rf, :], at_ref[1, rb, :]], axis=0)
        o = _dot(attn, v16) + jnp.concatenate([pf[c:], pb[c:]], axis=0)
        of_ref[rf, :] = o[:c]
        ob_ref[rb, :] = o[c:]
        cdf = cd_ref[0, pl.ds(pl.multiple_of(i * 8, 8), 1), :]
        cdb = cd_ref[1, pl.ds(pl.multiple_of(ib * 8, 8), 1), :]
        sf = sf * cdf + _dot_tn(kd_ref[0, rf, :], v16[:c])
        sb = sb * cdb + _dot_tn(kd_ref[1, rb, :], v16[c:])
        return sf, sb

    def prep_and_steps(it, st):
        for j in range(per_end):
            st = step((it - 1) * per_end + j, st)
        prep(it)
        return st

    zero = jnp.zeros((dk, dk), F32)
    n_prep = n // unroll
    prep(jnp.int32(0))
    st = lax.fori_loop(1, n_prep, prep_and_steps, (zero, zero))
    lax.fori_loop((n_prep - 1) * per_end, n, step, st, unroll=2)

    def finish(ci, carry):
        rows = pl.ds(pl.multiple_of(ci * c, c), c)
        z = z_ref[0, rows, :]
        r = _rms(of_ref[rows, :] + ob_ref[rows, :], gn_ref[...]) * (z * jax.nn.sigmoid(z))
        o_ref[0, rows, :] = r.astype(o_ref.dtype)
        return carry

    lax.fori_loop(0, n, finish, 0, unroll=4)


def gated_deltanet_core(proj, conv_w, a_log, dt_bias, onorm_g):
    b, l, _ = proj.shape
    d = conv_w.shape[1] // 3
    dk = DN_HEAD_DIM
    nh = d // dk
    n = l // DN_CHUNK
    assert l % DN_CHUNK == 0 and dk == LANES == 2 * DN_CHUNK
    unroll = 8
    assert n % unroll == 0
    hp =jnp.concatenate([a_log, dt_bias], axis=0).astype(F32)
    tok = lambda off: pl.BlockSpec((1, l, dk), lambda bi, h: (bi, 0, off + h))
    cw = lambda off: pl.BlockSpec((DN_CONV, dk), lambda bi, h: (0, off + h))
    return pl.pallas_call(
        functools.partial(_dn_kernel, nheads=nh, unroll=unroll),
        grid=(b, nh),
        in_specs=[
            pl.BlockSpec(hp.shape, lambda bi, h: (0, 0)),
            pl.BlockSpec((1, dk), lambda bi, h: (0, 0)),
            cw(0), cw(nh), cw(2 * nh),
            tok(0), tok(nh), tok(2 * nh), tok(3 * nh),
            pl.BlockSpec((1, l, dk), lambda bi, h: (bi, 0, 4 * nh)),
        ],
        out_specs=pl.BlockSpec((1, l, dk), lambda bi, h: (bi, 0, h)),
        out_shape=jax.ShapeDtypeStruct((b, l, d), BF16),
        scratch_shapes=[
            pltpu.VMEM((2, l, dk), F32), pltpu.VMEM((2, 2 * l, dk), BF16), pltpu.VMEM((2, l, dk), BF16),
            pltpu.VMEM((2, l, 2 * DN_CHUNK), BF16), pltpu.VMEM((2, n * 8, dk), F32),
            pltpu.VMEM((l, dk), F32), pltpu.VMEM((l, dk), F32),
        ],
        compiler_params=_cparams(("parallel", "parallel")),
        name="gated_deltanet",
    )(hp, onorm_g.reshape(1, dk), conv_w, conv_w, conv_w, proj, proj, proj, proj, proj)


def _rope_tables(l, width):
    half = DA_HEAD_DIM // 2
    inv = jnp.power(ROPE_THETA, -jnp.arange(half, dtype=F32) * 2.0 / DA_HEAD_DIM)
    ang = jnp.arange(l, dtype=F32)[:, None] * inv[None, :]
    cos, sin = jnp.cos(ang), jnp.sin(ang)
    reps = width // DA_HEAD_DIM
    return (jnp.tile(jnp.concatenate([cos, cos], axis=-1), (1, reps)),
            jnp.tile(jnp.concatenate([-sin, sin], axis=-1), (1, reps)))


def _pad_cols(w, mult):
    pad = -w.shape[1] % mult
    return jnp.pad(w, ((0, 0), (0, pad))) if pad else w


def _layer(x, mod_i, p, i, final):
    b, l, d = x.shape
    tm = min(1024, l)
    tn = 512
    tf = 256
    m, j = i % N_MIXERS, i // N_MIXERS
    sh1, sc1, g1, sh2, sc2, g2 = [mod_i[:, None, k * d:(k + 1) * d] for k in range(6)]
    n1 = p['norm1_g'][i]
    mode = "linear"
    if m == 0:
        lambda_init = 0.8 - 0.6 * math.exp(-0.3 * i)
        qkv = nm_matmul(x, n1, sc1, sh1, p['da_w_in'][j].astype(BF16), BF16, tm, tn,
                        rope=_rope_tables(l, tn) + (2 * d,))
        mix = diff_attention_core(qkv, p['da_lam'][j].astype(F32), p['da_subln_g'][j], lambda_init, 128, 1024)
        wout = p['da_w_out'][j]
    elif m == 1:
        u = nm_matmul(x, n1, sc1, sh1, p['s5_w_in'][j].astype(BF16), F32, tm, tn)
        ops = s5_operators(p['s5_a_re'][j], p['s5_a_im'][j], p['s5_log_dt'][j], p['s5_b_re'][j],
                           p['s5_b_im'][j], p['s5_c_re'][j], p['s5_c_im'][j])
        mix = (s5_core(u, ops), u, p['s5_d'][j])
        wout = p['s5_w_glu'][j]
        mode = "glu"
    elif m == 2:
        qkv = nm_matmul(x, n1, sc1, sh1, p['na_w_in'][j].astype(BF16), BF16, tm, tn)
        mix = neighborhood_attention_core(qkv, p['na_rpb'][j])
        wout = p['na_w_out'][j]
    else:
        proj = nm_matmul(x, n1, sc1, sh1, _pad_cols(p['dn_w_in'][j], tn).astype(BF16), F32, tm, tn)
        mix = gated_deltanet_core(proj, p['dn_conv_w'][j], p['dn_a_log'][j], p['dn_dt_bias'][j],
                                  p['dn_onorm_g'][j])
        wout = p['dn_w_out'][j]
    return out_ffn(x, mix, wout.astype(BF16), g1, p['norm2_g'][i], sc2, sh2, g2,
                   p['ffn_w1'][i].astype(BF16), p['ffn_w3'][i].astype(BF16), p['ffn_w2'][i].astype(BF16),
                   p['final_g'] if final else None, mode, tm, tf)


def _encoder_trunk(x, mod, p):
    depth = p['ffn_w1'].shape[0]
    for i in range(depth):
        x = _layer(x, mod[i], p, i, i == depth - 1)
    return x


def kernel(x_prompt, x_sample, c_prompt, c_sample, ada_w, ada_b, norm1_g, norm2_g, ffn_w1, ffn_w3, ffn_w2, da_w_in, da_lam, da_subln_g, da_w_out, s5_w_in, s5_a_re, s5_a_im, s5_log_dt, s5_b_re, s5_b_im, s5_c_re, s5_c_im, s5_d, s5_w_glu, na_w_in, na_rpb, na_w_out, dn_w_in, dn_conv_w, dn_a_log, dn_dt_bias, dn_onorm_g, dn_w_out, final_g):
    p = dict(norm1_g=norm1_g, norm2_g=norm2_g, ffn_w1=ffn_w1, ffn_w3=ffn_w3, ffn_w2=ffn_w2,
             da_w_in=da_w_in, da_lam=da_lam, da_subln_g=da_subln_g, da_w_out=da_w_out,
             s5_w_in=s5_w_in, s5_a_re=s5_a_re, s5_a_im=s5_a_im, s5_log_dt=s5_log_dt,
             s5_b_re=s5_b_re, s5_b_im=s5_b_im, s5_c_re=s5_c_re, s5_c_im=s5_c_im, s5_d=s5_d, s5_w_glu=s5_w_glu,
             na_w_in=na_w_in, na_rpb=na_rpb, na_w_out=na_w_out,
             dn_w_in=dn_w_in, dn_conv_w=dn_conv_w, dn_a_log=dn_a_log, dn_dt_bias=dn_dt_bias,
             dn_onorm_g=dn_onorm_g, dn_w_out=dn_w_out, final_g=final_g)
    nb = x_prompt.shape[0]
    mod = adaln(jnp.concatenate([c_prompt, c_sample], axis=0), ada_w, ada_b)
    y_prompt = _encoder_trunk(x_prompt, mod[:, :nb], p)
    y_sample = _encoder_trunk(x_sample, mod[:, nb:], p)
    return (y_prompt, y_sample)
```

```python
import functools
import math

import jax
import jax.numpy as jnp
from jax import lax
from jax.experimental import pallas as pl
from jax.experimental.pallas import tpu as pltpu

F32 = jnp.float32
BF16 = jnp.bfloat16
HI = lax.Precision.HIGHEST

EPS = 1e-6
ROPE_THETA = 10000.0
N_MIXERS = 4
GRID_W = 64
DA_HEAD_DIM = 64
Q_BLOCK = 128
S5_GROUP = 16
S5_STATE = 64
S5_CHUNK = 16
NA_HEAD_DIM = 64
NA_WIN_ROWS = 8
NA_WIN_COLS = 16
DN_HEAD_DIM = 128
DN_CONV = 4
DN_CHUNK = 64

V7X_VMEM_LIMIT = 52 * 1024 * 1024
LANES = 128
MM_SUB_ROWS = 512
MM_SUB_COLS = 256


def _cparams(sem):
    return pltpu.CompilerParams(dimension_semantics=sem, vmem_limit_bytes=V7X_VMEM_LIMIT)


def _rms(xf, g):
    return xf * lax.rsqrt(jnp.mean(xf * xf, axis=-1, keepdims=True) + EPS) * g


def _dot(a, b):
    return jnp.dot(a, b, preferred_element_type=F32)


def _dot_nt(a, b):
    return lax.dot_general(a, b, (((1,), (1,)), ((), ())), preferred_element_type=F32)


def _dot_tn(a, b):
    return lax.dot_general(a, b, (((0,), (0,)), ((), ())), preferred_element_type=F32)


def _adaln_kernel(c_ref, w_ref, b_ref, o_ref):
    c = c_ref[...]
    a = c * jax.nn.sigmoid(c)
    o_ref[0] = jnp.dot(a, w_ref[0], preferred_element_type=F32, precision=HI) + b_ref[0]


def adaln(c, ada_w, ada_b):
    depth, d, n = ada_w.shape
    rows = c.shape[0]
    tn = 1536 if n % 1536 == 0 else n
    return pl.pallas_call(
        _adaln_kernel,
        grid=(depth, n // tn),
        in_specs=[
            pl.BlockSpec((rows, d), lambda i, j: (0, 0)),
            pl.BlockSpec((1, d, tn), lambda i, j: (i, 0, j)),
            pl.BlockSpec((1, 1, tn), lambda i, j: (i, 0, j)),
        ],
        out_specs=pl.BlockSpec((1, rows, tn), lambda i, j: (i, 0, j)),
        out_shape=jax.ShapeDtypeStruct((depth, rows, n), F32),
        compiler_params=_cparams(("parallel", "parallel")),
        name="adaln",
    )(c, ada_w, ada_b.reshape(depth, 1, n))


def _swap_halves(a, half):
    n = a.shape[-1]
    lane = lax.broadcasted_iota(jnp.int32, a.shape, a.ndim - 1)
    first = (lane % (2 * half)) < half
    return jnp.where(first, pltpu.roll(a, n - half, a.ndim - 1), pltpu.roll(a, half, a.ndim - 1))


def _nm_matmul_kernel(x_ref, g_ref, sc_ref, sh_ref, w_ref, *rest, rope_cols):
    if rope_cols:
        cos_ref, sin_ref, o_ref, h_ref = rest
    else:
        o_ref, h_ref = rest
    h = _rms(x_ref[0], g_ref[...]) * (1.0 + sc_ref[0]) + sh_ref[0]
    h_ref[...] = h.astype(BF16)
    tm, n = o_ref.shape[1], o_ref.shape[2]
    sm, sn = min(tm, MM_SUB_ROWS), min(n, MM_SUB_COLS)
    for r0 in range(0, tm, sm):
        for c0 in range(0, n, sn):
            rs, cs = slice(r0, r0 + sm), slice(c0, c0 + sn)
            acc = _dot(h_ref[rs, :], w_ref[:, cs])
            if c0 < rope_cols:
                acc = acc * cos_ref[rs, :] + _swap_halves(acc, DA_HEAD_DIM // 2) * sin_ref[rs, :]
            o_ref[0, rs, cs] = acc.astype(o_ref.dtype)


def nm_matmul(x, g, sc, sh, w, out_dtype, tm, rope=None):
    b, l, d = x.shape
    n = w.shape[1]
    assert l % tm == 0 and n % min(n, MM_SUB_COLS) == 0
    in_specs = [
        pl.BlockSpec((1, tm, d), lambda bi, i: (bi, i, 0)),
        pl.BlockSpec((1, d), lambda bi, i: (0, 0)),
        pl.BlockSpec((1, 1, d), lambda bi, i: (bi, 0, 0)),
        pl.BlockSpec((1, 1, d), lambda bi, i: (bi, 0, 0)),
        pl.BlockSpec((d, n), lambda bi, i: (0, 0), pipeline_mode=pl.Buffered(1)),
    ]
    args = [x, g.reshape(1, d), sc, sh, w]
    rope_cols = 0
    if rope is not None:
        cos_t, sin_t, rope_cols = rope
        assert rope_cols % MM_SUB_COLS == 0 and cos_t.shape[1] == MM_SUB_COLS
        in_specs += [pl.BlockSpec((tm, MM_SUB_COLS), lambda bi, i: (i, 0))] * 2
        args += [cos_t, sin_t]
    return pl.pallas_call(
        functools.partial(_nm_matmul_kernel, rope_cols=rope_cols),
        grid=(b, l // tm),
        in_specs=in_specs,
        out_specs=pl.BlockSpec((1, tm, n), lambda bi, i: (bi, i, 0)),
        out_shape=jax.ShapeDtypeStruct((b, l, n), out_dtype),
        scratch_shapes=[pltpu.VMEM((tm, d), BF16)],
        compiler_params=_cparams(("parallel", "parallel")),
        name="nm_matmul",
    )(*args)


def _out_ffn_kernel(*refs, mode, final):
    it = iter(refs)
    x_ref = next(it)
    if mode == "glu":
        ys_ref, u_ref, dsk_ref = next(it), next(it), next(it)
    else:
        o_ref = next(it)
    wout_ref, g1_ref, gn_ref, sc_ref, sh_ref, g2_ref = (next(it) for _ in range(6))
    w1_ref, w3_ref, w2_ref = next(it), next(it), next(it)
    fg_ref = next(it) if final else None
    out_ref, x1_ref, h_ref, acc_ref = next(it), next(it), next(it), next(it)
    f = pl.program_id(2)
    d = x_ref.shape[-1]

    tm = x_ref.shape[1]
    sm, sn = min(tm, MM_SUB_ROWS), min(d, MM_SUB_COLS)
    row_blocks = [slice(r0, r0 + sm) for r0 in range(0, tm, sm)]

    @pl.when(f == 0)
    def _():
        for rs in row_blocks:
            if mode == "glu":
                o = jax.nn.gelu(ys_ref[0, rs, :] + dsk_ref[...] * u_ref[0, rs, :]).astype(BF16)
            else:
                o = o_ref[0, rs, :]
            for c0 in range(0, d, sn):
                cs = slice(c0, c0 + sn)
                y = _dot(o, wout_ref[:, cs])
                if mode == "glu":
                    y = y * jax.nn.sigmoid(_dot(o, wout_ref[:, slice(d + c0, d + c0 + sn)]))
                x1_ref[rs, cs] = x_ref[0, rs, cs] + g1_ref[0, :, cs] * y
            h = _rms(x1_ref[rs, :], gn_ref[...]) * (1.0 + sc_ref[0]) + sh_ref[0]
            h_ref[rs, :] = h.astype(BF16)
        acc_ref[...] = jnp.zeros_like(acc_ref)

    for rs in row_blocks:
        h = h_ref[rs, :]
        for c0 in range(0, w1_ref.shape[1], sn):
            cs = slice(c0, c0 + sn)
            a = _dot(h, w1_ref[:, cs])
            b = _dot(h, w3_ref[:, cs])
            t = (a * jax.nn.sigmoid(a) * b).astype(BF16)
            acc_ref[rs, :] += _dot(t, w2_ref[cs, :])

    @pl.when(f == pl.num_programs(2) - 1)
    def _():
        r = x1_ref[...] + g2_ref[0] * acc_ref[...]
        if final:
            r = _rms(r, fg_ref[...])
        out_ref[0] = r


def out_ffn(x, mix_in, wout, g1, gn, sc2, sh2, g2, w1, w3, w2, final_g, mode, tm, tf):
    b, l, d = x.shape
    fh = w1.shape[1]
    assert l % tm == 0 and fh % tf == 0
    wmode = pl.Buffered(1) if tf == fh else None
    tok = pl.BlockSpec((1, tm, d), lambda bi, i, f: (bi, i, 0))
    per_b = pl.BlockSpec((1, 1, d), lambda bi, i, f: (bi, 0, 0))
    vec = pl.BlockSpec((1, d), lambda bi, i, f: (0, 0))
    in_specs = [tok]
    args = [x]
    if mode == "glu":
        ys, u, dsk = mix_in
        in_specs += [tok, tok, vec]
        args += [ys, u, dsk.reshape(1, d)]
    else:
        in_specs += [tok]
        args += [mix_in]
    in_specs += [pl.BlockSpec(wout.shape, lambda bi, i, f: (0, 0), pipeline_mode=pl.Buffered(1)),
                 per_b, vec, per_b, per_b, per_b,
                 pl.BlockSpec((d, tf), lambda bi, i, f: (0, f), pipeline_mode=wmode),
                 pl.BlockSpec((d, tf), lambda bi, i, f: (0, f), pipeline_mode=wmode),
                 pl.BlockSpec((tf, d), lambda bi, i, f: (f, 0), pipeline_mode=wmode)]
    args += [wout, g1, gn.reshape(1, d), sc2, sh2, g2, w1, w3, w2]
    final = final_g is not None
    if final:
        in_specs += [vec]
        args += [final_g.reshape(1, d)]
    return pl.pallas_call(
        functools.partial(_out_ffn_kernel, mode=mode, final=final),
        grid=(b, l // tm, fh // tf),
        in_specs=in_specs,
        out_specs=tok,
        out_shape=jax.ShapeDtypeStruct((b, l, d), F32),
        scratch_shapes=[pltpu.VMEM((tm, d), F32), pltpu.VMEM((tm, d), BF16), pltpu.VMEM((tm, d), F32)],
        compiler_params=_cparams(("parallel", "parallel", "arbitrary")),
        name="out_ffn",
    )(*args)


def _block_diag_rows(q):
    lane = lax.broadcasted_iota(jnp.int32, q.shape, 1)
    zero = jnp.zeros_like(q)
    return jnp.concatenate([jnp.where(lane < 64, q, zero), jnp.where(lane >= 64, q, zero)], axis=0)


def _da_kernel(lam_ref, g_ref, q_ref, k_ref, v_ref, o_ref, *, tq, kc, lambda_init):
    l = q_ref.shape[1]
    lam = lam_ref[...]
    lam_full = (jnp.exp(jnp.sum(lam[0:1] * lam[1:2], axis=-1, keepdims=True))
                - jnp.exp(jnp.sum(lam[2:3] * lam[3:4], axis=-1, keepdims=True)) + lambda_init)
    scale = DA_HEAD_DIM ** -0.5

    def q_block(qi, carry):
        q = q_ref[0, pl.ds(pl.multiple_of(qi * tq, tq), tq), :]
        qbd = _block_diag_rows(q * scale)

        nk = l // kc
        scores = lambda ki: _dot_nt(qbd, k_ref[0, pl.ds(pl.multiple_of(ki * kc, kc), kc), :])

        def kv_step(ki, st):
            s, m, s_sum, acc = st
            s_next = scores(jnp.minimum(ki + 1, nk - 1))
            m_new = jnp.maximum(m, jnp.max(s, axis=-1, keepdims=True))
            alpha = jnp.exp(m - m_new)
            e = jnp.exp(s - m_new)
            s_sum = alpha * s_sum + jnp.sum(e, axis=-1, keepdims=True)
            acc = alpha * acc + _dot(e.astype(BF16), v_ref[0, pl.ds(pl.multiple_of(ki * kc, kc), kc), :])
            return s_next, m_new, s_sum, acc

        init = (scores(0), jnp.full((2 * tq, 1), -1e30, F32), jnp.zeros((2 * tq, 1), F32),
                jnp.zeros((2 * tq, 2 * DA_HEAD_DIM), F32))
        _, _, s_sum, acc = lax.fori_loop(0, nk, kv_step, init, unroll=True)
        o = acc / s_sum
        a = o[:tq] - lam_full * o[tq:]
        r = _rms(a, g_ref[...]) * (1.0 - lambda_init)
        o_ref[0, pl.ds(pl.multiple_of(qi * tq, tq), tq), :] = r.astype(o_ref.dtype)
        return carry

    lax.fori_loop(0, l // tq, q_block, 0, unroll=2)


def diff_attention_core(qkv, lam, subln_g, lambda_init, tq, kc):
    b, l, n3 = qkv.shape
    d = n3 // 3
    hd = 2 * DA_HEAD_DIM
    nh = d // hd
    tq, kc = min(tq, l), min(kc, l)
    assert l % tq == 0 and l % kc == 0
    return pl.pallas_call(
        functools.partial(_da_kernel, tq=tq, kc=kc, lambda_init=lambda_init),
        grid=(b, nh),
        in_specs=[
            pl.BlockSpec(lam.shape, lambda bi, h: (0, 0)),
            pl.BlockSpec((1, hd), lambda bi, h: (0, 0)),
            pl.BlockSpec((1, l, hd), lambda bi, h: (bi, 0, h)),
            pl.BlockSpec((1, l, hd), lambda bi, h: (bi, 0, nh + h)),
            pl.BlockSpec((1, l, hd), lambda bi, h: (bi, 0, 2 * nh + h)),
        ],
        out_specs=pl.BlockSpec((1, l, hd), lambda bi, h: (bi, 0, h)),
        out_shape=jax.ShapeDtypeStruct((b, l, d), BF16),
        compiler_params=_cparams(("parallel", "parallel")),
        name="diff_attn",
    )(lam, subln_g.reshape(1, hd), qkv, qkv, qkv)


def _na_kernel(q_ref, k_ref, v_ref, t2_ref, o_ref, *, rows, pairs, unroll):
    kr = NA_WIN_ROWS
    kw = kr * GRID_W
    scale = NA_HEAD_DIM ** -0.5
    rr = lax.broadcasted_iota(jnp.int32, (2 * GRID_W, 2 * GRID_W), 0) % GRID_W
    cc = lax.broadcasted_iota(jnp.int32, (2 * GRID_W, 2 * GRID_W), 1) % GRID_W
    cs = jnp.clip(rr - NA_WIN_COLS // 2, 0, GRID_W - NA_WIN_COLS)
    valid2 = (cc >= cs) & (cc < cs + NA_WIN_COLS)
    valid = jnp.concatenate([valid2] * (kr // 2), axis=1)
    lane = lax.broadcasted_iota(jnp.int32, (GRID_W, 2 * NA_HEAD_DIM), 1)

    def row_step(it, carry):
        jobs = []
        for u in range(unroll):
            r = it * unroll + u
            rs = jnp.clip(r - kr // 2, 0, rows - kr)
            for p in range(pairs):
                jobs.append((pl.ds(pl.multiple_of(r * GRID_W, GRID_W), GRID_W),
                             pl.ds(pl.multiple_of(rs * GRID_W, GRID_W), kw),
                             slice(p * 128, (p + 1) * 128), p, rs - r + NA_WIN_ROWS - 1))
        s = [_dot_nt(_block_diag_rows(q_ref[0, qs, cols]), k_ref[0, bs, cols]) * scale
             for qs, bs, cols, _, _ in jobs]
        bias = [jnp.concatenate([t2_ref[p, rho0 + 2 * j] for j in range(kr // 2)], axis=1)
                for _, _, _, p, rho0 in jobs]
        s = [jnp.where(valid, x + y, -1e30) for x, y in zip(s, bias)]
        m = [jnp.max(x, axis=-1, keepdims=True) for x in s]
        e = [jnp.exp(x - y) for x, y in zip(s, m)]
        den = [jnp.sum(x, axis=-1, keepdims=True) for x in e]
        o = [_dot(x.astype(BF16), v_ref[0, bs, cols]) / y for x, y, (_, bs, cols, _, _) in zip(e, den, jobs)]
        for x, (qs, _, cols, _, _) in zip(o, jobs):
            o_ref[0, qs, cols] = jnp.where(lane < NA_HEAD_DIM, x[:GRID_W], x[GRID_W:]).astype(o_ref.dtype)
        return carry

    lax.fori_loop(0, rows // unroll, row_step, 0)


def na_bias_table(rpb):
    nh = rpb.shape[0]
    cols = jnp.arange(GRID_W)
    cidx = jnp.clip(cols[None, :] - cols[:, None], -(NA_WIN_COLS - 1), NA_WIN_COLS - 1) + NA_WIN_COLS - 1
    t = rpb[:, :, cidx]
    t = t.reshape(nh // 2, 2, 2 * NA_WIN_ROWS - 1, GRID_W, GRID_W).transpose(0, 2, 1, 3, 4)
    t = t.reshape(nh // 2, 2 * NA_WIN_ROWS - 1, 2 * GRID_W, GRID_W)
    return jnp.concatenate([t[:, :-1], t[:, 1:]], axis=-1).astype(F32)


def neighborhood_attention_core(qkv, rpb):
    b, l, n3 = qkv.shape
    d = n3 // 3
    rows = l // GRID_W
    assert rows >= NA_WIN_ROWS and l % GRID_W == 0
    pairs = 2
    gw = pairs * 2 * NA_HEAD_DIM
    ng = d // gw
    t2 = na_bias_table(rpb)
    nrho = t2.shape[1]
    return pl.pallas_call(
        functools.partial(_na_kernel, rows=rows, pairs=pairs, unroll=4 if rows % 4 == 0 else 1),
        grid=(b, ng),
        in_specs=[
            pl.BlockSpec((1, l, gw), lambda bi, g: (bi, 0, g)),
            pl.BlockSpec((1, l, gw), lambda bi, g: (bi, 0, ng + g)),
            pl.BlockSpec((1, l, gw), lambda bi, g: (bi, 0, 2 * ng + g)),
            pl.BlockSpec((pairs, nrho, 2 * GRID_W, 2 * GRID_W), lambda bi, g: (g, 0, 0, 0)),
        ],
        out_specs=pl.BlockSpec((1, l, gw), lambda bi, g: (bi, 0, g)),
        out_shape=jax.ShapeDtypeStruct((b, l, d), BF16),
        compiler_params=_cparams(("parallel", "parallel")),
        name="nbr_attn",
    )(qkv, qkv, qkv, t2)


def s5_operators(a_re, a_im, log_dt, b_re, b_im, c_re, c_im):
    t = S5_CHUNK
    lam = lax.complex(a_re.astype(F32), a_im.astype(F32))
    dt = jnp.exp(log_dt.astype(F32))[..., None]
    lam_bar = jnp.exp(lam * dt)
    b_bar = ((lam_bar - 1.0) / lam)[..., None] * lax.complex(b_re.astype(F32), b_im.astype(F32))
    c_mat = lax.complex(c_re.astype(F32), c_im.astype(F32))
    taus = jnp.arange(t + 1, dtype=F32)
    pw = jnp.exp((lam * dt)[:, :, None, :] * taus[None, None, :, None])
    kern = jnp.real(jnp.einsum('dgpn,dgtn,dgnq->dgtpq', c_mat, pw[:, :, :t], b_bar, precision=HI))
    ti = jnp.arange(t)
    tau_f = ti[None, :] - ti[:, None]
    kf = jnp.where((tau_f >= 0)[None, :, :, None, None], kern[0][:, jnp.clip(tau_f, 0, t - 1)], 0.0)
    kb = jnp.where((tau_f <= 0)[None, :, :, None, None], kern[1][:, jnp.clip(-tau_f, 0, t - 1)], 0.0)
    g, p = kern.shape[1], kern.shape[3]
    wy_u = (kf + kb).transpose(0, 1, 4, 2, 3).reshape(g, t * p, t * p)

    def state_to_y(dr, powers):
        m = c_mat[dr][:, None, :, :] * pw[dr][:, powers, None, :]
        w = jnp.concatenate([jnp.real(m), -jnp.imag(m)], axis=-1)
        return w.transpose(0, 3, 1, 2).reshape(g, -1, t * p)

    wy_x = jnp.concatenate([state_to_y(0, ti + 1), state_to_y(1, t - ti)], axis=1)

    def u_to_state(dr, powers):
        m = pw[dr][:, powers, :, None] * b_bar[dr][:, None, :, :]
        w = jnp.concatenate([jnp.real(m), jnp.imag(m)], axis=2)
        return w.transpose(0, 1, 3, 2).reshape(g, t * p, -1)

    wb = jnp.concatenate([u_to_state(0, t - 1 - ti), u_to_state(1, ti)], axis=-1)
    a16 = pw[:, :, t, :]
    rot = jnp.stack([jnp.concatenate([jnp.real(a16), jnp.real(a16)], -1),
                     jnp.concatenate([-jnp.imag(a16), jnp.imag(a16)], -1)], axis=2)
    rot = rot.transpose(1, 0, 2, 3).reshape(g, 4, -1)
    return wb.astype(BF16), wy_u.astype(BF16), wy_x.astype(BF16), rot.astype(F32)


def _s5_kernel(u_ref, wb_ref, wyu_ref, wyx_ref, rot_ref, y_ref, s_ref, x_ref, *, nc, nb, gb):
    ns2 = 2 * S5_STATE
    for g in range(gb):
        s_ref[g] = _dot(u_ref[g], wb_ref[g])

    def step(c, xs):
        cb = nc - 1 - c
        rf = pl.ds(pl.multiple_of(c * nb, nb), nb)
        rb = pl.ds(pl.multiple_of(cb * nb, nb), nb)
        new = []
        for g in range(gb):
            xf, xb = xs[2 * g], xs[2 * g + 1]
            rot = rot_ref[g]
            x_ref[g, rf, 0:ns2] = xf.astype(BF16)
            x_ref[g, rb, ns2:2 * ns2] = xb.astype(BF16)
            xf = rot[0:1] * xf + rot[1:2] * pltpu.roll(xf, S5_STATE, 1) + s_ref[g, rf, 0:ns2]
            xb = rot[2:3] * xb + rot[3:4] * pltpu.roll(xb, S5_STATE, 1) + s_ref[g, rb, ns2:2 * ns2]
            new += [xf, xb]
        return tuple(new)

    lax.fori_loop(0, nc, step, tuple(jnp.zeros((nb, ns2), F32) for _ in range(2 * gb)))
    for g in range(gb):
        y_ref[g] = (_dot(u_ref[g], wyu_ref[g]) + _dot(x_ref[g], wyx_ref[g])).astype(y_ref.dtype)


def _s5_pack_kernel(u_ref, perm_ref, o_ref):
    w = o_ref.shape[2]
    for gl in range(o_ref.shape[0]):
        o_ref[gl] = _dot(u_ref[0], perm_ref[:, gl * w:(gl + 1) * w]).astype(o_ref.dtype)


def _s5_unpack_kernel(y_ref, perm_t_ref, o_ref):
    ycat = jnp.concatenate([y_ref[gl] for gl in range(y_ref.shape[0])], axis=1)
    w = y_ref.shape[2]
    for c0 in range(0, o_ref.shape[2], w):
        o_ref[0, :, c0:c0 + w] = _dot(ycat, perm_t_ref[:, c0:c0 + w]).astype(o_ref.dtype)


def _s5_relayout(x, perm, pack):
    gl = LANES // S5_GROUP
    w = S5_CHUNK * S5_GROUP
    rows = x.shape[1]
    nj = x.shape[0] if pack else x.shape[0] // gl
    tr = min(MM_SUB_ROWS, rows)
    assert rows % tr == 0
    wide = pl.BlockSpec((1, tr, gl * w), lambda j, i: (j, i, 0))
    narrow = pl.BlockSpec((gl, tr, w), lambda j, i: (j, i, 0))
    return pl.pallas_call(
        _s5_pack_kernel if pack else _s5_unpack_kernel,
        grid=(nj, rows // tr),
        in_specs=[wide if pack else narrow,
                  pl.BlockSpec(perm.shape, lambda j, i: (0, 0), pipeline_mode=pl.Buffered(1))],
        out_specs=narrow if pack else wide,
        out_shape=jax.ShapeDtypeStruct((nj * gl, rows, w) if pack else (nj, rows, gl * w), BF16),
        compiler_params=_cparams(("parallel", "parallel")),
        name="s5_pack" if pack else "s5_unpack",
    )(x, perm)


def _s5_lane_perm():
    t, gl, p = S5_CHUNK, LANES // S5_GROUP, S5_GROUP
    src = jnp.arange(t * gl * p).reshape(t, gl, p).transpose(1, 0, 2).reshape(-1)
    return (jnp.arange(t * gl * p)[:, None] == src[None, :]).astype(BF16)


def s5_core(u, ops, gb=2):
    b, l, d = u.shape
    t, p = S5_CHUNK, S5_GROUP
    g = d // p
    nc = l // t
    w = t * p
    nj = d // LANES
    wb, wy_u, wy_x, rot = ops
    perm = _s5_lane_perm()
    uj = u.astype(BF16).reshape(b, nc, t, nj, LANES).transpose(3, 1, 0, 2, 4).reshape(nj, nc * b, t * LANES)
    ur = _s5_relayout(uj, perm, pack=True)
    grp = lambda shape: pl.BlockSpec((gb,) + shape, lambda i: (i, 0, 0))
    yr = pl.pallas_call(
        functools.partial(_s5_kernel, nc=nc, nb=b, gb=gb),
        grid=(g // gb,),
        in_specs=[grp((nc * b, w)), grp(wb.shape[1:]), grp(wy_u.shape[1:]), grp(wy_x.shape[1:]), grp(rot.shape[1:])],
        out_specs=grp((nc * b, w)),
        out_shape=jax.ShapeDtypeStruct((g, nc * b, w), BF16),
        scratch_shapes=[pltpu.VMEM((gb, nc * b, 4 * S5_STATE), F32), pltpu.VMEM((gb, nc * b, 4 * S5_STATE), BF16)],
        compiler_params=_cparams(("parallel",)),
        name="s5_scan",
    )(ur, wb, wy_u, wy_x, rot)
    yj = _s5_relayout(yr, perm.T, pack=False)
    return yj.reshape(nj, nc, b, t, LANES).transpose(2, 1, 3, 0, 4).reshape(b, l, d)


def _split2(a):
    hi = a.astype(BF16)
    return hi, (a - hi.astype(F32)).astype(BF16)


def _dot3(a, b):
    (ah, al), (bh, bl) = a, b
    return _dot(ah, bh) + (_dot(ah, bl) + _dot(al, bh))


def _tri_inverse_minus_eye(ms):
    n = [-m for m in ms]
    sa = [_split2(x) for x in n]
    for _ in range(5):
        a = [_dot3(x, x) for x in sa]
        sa = [_split2(x) for x in a]
        n = [y + x + _dot3(_split2(y), sx) for x, y, sx in zip(a, n, sa)]
    return n


def _dn_kernel(hp_ref, gn_ref, wq_ref, wk_ref, wv_ref, q_ref, k_ref, v_ref, z_ref, ab_ref, o_ref,
               u_ref, wq2_ref, kd_ref, at_ref, cd_ref, of_ref, ob_ref, *, nheads, unroll):
    l = q_ref.shape[1]
    c = DN_CHUNK
    c2 = 2 * c
    n = l // c
    dk = DN_HEAD_DIM
    h = pl.program_id(1)
    lane_hp = lax.broadcasted_iota(jnp.int32, hp_ref.shape, 1)
    hp = jnp.sum(jnp.where(lane_hp == h, hp_ref[...], 0.0), axis=-1, keepdims=True)
    row = lax.broadcasted_iota(jnp.int32, (c2, c2), 0)
    col = lax.broadcasted_iota(jnp.int32, (c2, c2), 1)
    fwd = row < c
    same = fwd == (col < c)
    incl = same & ((fwd & (row >= col)) | (jnp.logical_not(fwd) & (row <= col)))
    strict = incl & (row != col)
    tri16 = incl.astype(F32).astype(BF16)
    fwd_col = fwd[:, 0:1]
    a_log = jnp.where(fwd_col, hp[0:1], hp[1:2])
    dt_b = jnp.where(fwd_col, hp[2:3], hp[3:4])
    lane_ab = lax.broadcasted_iota(jnp.int32, (c, LANES), 1)

    def conv_silu(x_ref, w_ref_, ci):
        base = pl.multiple_of(ci * c, c)
        x = x_ref[0, pl.ds(base, c), :]
        prev = x_ref[0, pl.ds(jnp.maximum(base - 8, 0), 8), :] * jnp.where(ci > 0, 1.0, 0.0)
        nxt = x_ref[0, pl.ds(jnp.minimum(base + c, l - 8), 8), :] * jnp.where(ci < n - 1, 1.0, 0.0)
        e = jnp.concatenate([prev, x, nxt], axis=0)
        ne = c + 16
        w = w_ref_[...]
        y = (w[0:1] * pltpu.roll(e, 1, 0) + w[1:2] * e
             + w[2:3] * pltpu.roll(e, ne - 1, 0) + w[3:4] * pltpu.roll(e, ne - 2, 0))[8:8 + c]
        return y * jax.nn.sigmoid(y)

    def pick(ci, j):
        ab = ab_ref[0, pl.ds(pl.multiple_of(ci * c, c), c), :]
        return jnp.sum(jnp.where(lane_ab == j * nheads + h, ab, 0.0), axis=-1, keepdims=True)

    lane = lax.broadcasted_iota(jnp.int32, (c2, dk), 1)

    per_end = unroll // 2

    def prep(it):
        cis = [it * per_end + j for j in range(per_end)] + [n - 1 - it * per_end - j for j in range(per_end)]
        two = lambda x: jnp.concatenate([x, x], axis=0)
        q = [conv_silu(q_ref, wq_ref, ci) for ci in cis]
        k = [conv_silu(k_ref, wk_ref, ci) for ci in cis]
        v = [two(conv_silu(v_ref, wv_ref, ci)) for ci in cis]
        q = [two(x * lax.rsqrt(jnp.sum(x * x, axis=-1, keepdims=True) + EPS) * (dk ** -0.5)) for x in q]
        k = [two(x * lax.rsqrt(jnp.sum(x * x, axis=-1, keepdims=True) + EPS)) for x in k]
        g_in = [jnp.concatenate([pick(ci, 0), pick(ci, 1)], axis=0) for ci in cis]
        beta = [jax.nn.sigmoid(jnp.concatenate([pick(ci, 2), pick(ci, 3)], axis=0)) for ci in cis]
        g = [-jnp.exp(a_log) * jax.nn.softplus(x + dt_b) for x in g_in]
        g_hi = [x.astype(BF16).astype(F32) for x in g]
        r1 = [x - y for x, y in zip(g, g_hi)]
        g_mid = [x.astype(BF16).astype(F32) for x in r1]
        g3 = [jnp.where(lane == 0, a_, jnp.where(lane == 1, b_, jnp.where(lane == 2, r_ - b_, 0.0)))
              for a_, b_, r_ in zip(g_hi, g_mid, r1)]
        gc = [jnp.broadcast_to(jnp.sum(_dot(tri16, x.astype(BF16)), axis=-1, keepdims=True), (c2, dk)) for x in g3]
        decay = [jnp.where(incl, jnp.exp(jnp.where(incl, x - x.T, 0.0)), 0.0) for x in gc]
        k16 = [x.astype(BF16) for x in k]
        kbeta = [x * y for x, y in zip(k, beta)]
        m = [jnp.where(strict, _dot_nt(x.astype(BF16), y) * z, 0.0) for x, y, z in zip(kbeta, k16, decay)]
        egc = [jnp.exp(x) for x in gc]
        rhs = [jnp.concatenate([x * y, z * w_], axis=1) for x, y, z, w_ in zip(v, beta, kbeta, egc)]
        ninv = _tri_inverse_minus_eye(m)
        sol = [x + _dot(y.astype(BF16), x.astype(BF16)) for x, y in zip(rhs, ninv)]
        attn = [(_dot_nt(x.astype(BF16), y) * z).astype(BF16) for x, y, z in zip(q, k16, decay)]
        for j, ci in enumerate(cis):
            gcj = gc[j]
            gc_last = jnp.where(fwd, gcj[c - 1:c], gcj[c:c + 1])
            qd = (q[j] * egc[j]).astype(BF16)
            w16 = sol[j][:, dk:].astype(BF16)
            kd = (k[j] * jnp.exp(gc_last - gcj)).astype(BF16)
            r1_ = pl.ds(pl.multiple_of(ci * c, c), c)
            r2_ = pl.ds(pl.multiple_of(ci * c2, c2), c2)
            r8_ = pl.ds(pl.multiple_of(ci * 8, 8), 8)
            for dr, half in enumerate((slice(0, c), slice(c, c2))):
                u_ref[dr, r1_, :] = sol[j][half, :dk]
                wq2_ref[dr, r2_, :] = jnp.concatenate([w16[half], qd[half]], axis=0)
                kd_ref[dr, r1_, :] = kd[half]
                at_ref[dr, r1_, :] = attn[j][half]
            cd_ref[0, r8_, :] = jnp.broadcast_to(jnp.exp(gcj[c - 1:c]), (8, dk))
            cd_ref[1, r8_, :] = jnp.broadcast_to(jnp.exp(gcj[c:c + 1]), (8, dk))

    def step(i, st):
        sf, sb = st
        ib = n - 1 - i
        rf = pl.ds(pl.multiple_of(i * c, c), c)
        rb = pl.ds(pl.multiple_of(ib * c, c), c)
        sf16, sb16 = sf.astype(BF16), sb.astype(BF16)
        pf = _dot(wq2_ref[0, pl.ds(pl.multiple_of(i * c2, c2), c2), :], sf16)
        pb = _dot(wq2_ref[1, pl.ds(pl.multiple_of(ib * c2, c2), c2), :], sb16)
        v_new = jnp.concatenate([u_ref[0, rf, :] - pf[:c], u_ref[1, rb, :] - pb[:c]], axis=0)
        v16 = v_new.astype(BF16)
        attn = jnp.concatenate([at_ref[0, rf, :], at_ref[1, rb, :]], axis=0)
        o = _dot(attn, v16) + jnp.concatenate([pf[c:], pb[c:]], axis=0)
        of_ref[rf, :] = o[:c]
        ob_ref[rb, :] = o[c:]
        cdf = cd_ref[0, pl.ds(pl.multiple_of(i * 8, 8), 1), :]
        cdb = cd_ref[1, pl.ds(pl.multiple_of(ib * 8, 8), 1), :]
        sf = sf * cdf + _dot_tn(kd_ref[0, rf, :], v16[:c])
        sb = sb * cdb + _dot_tn(kd_ref[1, rb, :], v16[c:])
        return sf, sb

    def prep_and_steps(it, st):
        for j in range(per_end):
            st = step((it - 1) * per_end + j, st)
        prep(it)
        return st

    zero = jnp.zeros((dk, dk), F32)
    n_prep = n // unroll
    prep(jnp.int32(0))
    st = lax.fori_loop(1, n_prep, prep_and_steps, (zero, zero))
    lax.fori_loop((n_prep - 1) * per_end, n, step, st, unroll=2)

    def finish(ci, carry):
        rows = pl.ds(pl.multiple_of(ci * c, c), c)
        z = z_ref[0, rows, :]
        r = _rms(of_ref[rows, :] + ob_ref[rows, :], gn_ref[...]) * (z * jax.nn.sigmoid(z))
        o_ref[0, rows, :] = r.astype(o_ref.dtype)
        return carry

    lax.fori_loop(0, n, finish, 0, unroll=4)


def gated_deltanet_core(proj, conv_w, a_log, dt_bias, onorm_g):
    b, l, _ = proj.shape
    d = conv_w.shape[1] // 3
    dk = DN_HEAD_DIM
    nh = d // dk
    n = l // DN_CHUNK
    assert l % DN_CHUNK == 0 and dk == LANES == 2 * DN_CHUNK
    unroll = 8
    assert n % unroll == 0
    hp =jnp.concatenate([a_log, dt_bias], axis=0).astype(F32)
    tok = lambda off: pl.BlockSpec((1, l, dk), lambda bi, h: (bi, 0, off + h))
    cw = lambda off: pl.BlockSpec((DN_CONV, dk), lambda bi, h: (0, off + h))
    return pl.pallas_call(
        functools.partial(_dn_kernel, nheads=nh, unroll=unroll),
        grid=(b, nh),
        in_specs=[
            pl.BlockSpec(hp.shape, lambda bi, h: (0, 0)),
            pl.BlockSpec((1, dk), lambda bi, h: (0, 0)),
            cw(0), cw(nh), cw(2 * nh),
            tok(0), tok(nh), tok(2 * nh), tok(3 * nh),
            pl.BlockSpec((1, l, dk), lambda bi, h: (bi, 0, 4 * nh)),
        ],
        out_specs=pl.BlockSpec((1, l, dk), lambda bi, h: (bi, 0, h)),
        out_shape=jax.ShapeDtypeStruct((b, l, d), BF16),
        scratch_shapes=[
            pltpu.VMEM((2, l, dk), F32), pltpu.VMEM((2, 2 * l, dk), BF16), pltpu.VMEM((2, l, dk), BF16),
            pltpu.VMEM((2, l, 2 * DN_CHUNK), BF16), pltpu.VMEM((2, n * 8, dk), F32),
            pltpu.VMEM((l, dk), F32), pltpu.VMEM((l, dk), F32),
        ],
        compiler_params=_cparams(("parallel", "parallel")),
        name="gated_deltanet",
    )(hp, onorm_g.reshape(1, dk), conv_w, conv_w, conv_w, proj, proj, proj, proj, proj)


def _rope_tables(l, width):
    half = DA_HEAD_DIM // 2
    inv = jnp.power(ROPE_THETA, -jnp.arange(half, dtype=F32) * 2.0 / DA_HEAD_DIM)
    ang = jnp.arange(l, dtype=F32)[:, None] * inv[None, :]
    cos, sin = jnp.cos(ang), jnp.sin(ang)
    reps = width // DA_HEAD_DIM
    return (jnp.tile(jnp.concatenate([cos, cos], axis=-1), (1, reps)),
            jnp.tile(jnp.concatenate([-sin, sin], axis=-1), (1, reps)))


def _pad_cols(w, mult):
    pad = -w.shape[1] % mult
    return jnp.pad(w, ((0, 0), (0, pad))) if pad else w


def _layer(x, mod_i, p, i, final):
    b, l, d = x.shape
    tm = min(512, l)
    tm_ffn = min(512, l)
    tf = p['ffn_w1'].shape[2]
    m, j = i % N_MIXERS, i // N_MIXERS
    sh1, sc1, g1, sh2, sc2, g2 = [mod_i[:, None, k * d:(k + 1) * d] for k in range(6)]
    n1 = p['norm1_g'][i]
    mode = "linear"
    if m == 0:
        lambda_init = 0.8 - 0.6 * math.exp(-0.3 * i)
        qkv = nm_matmul(x, n1, sc1, sh1, p['da_w_in'][j].astype(BF16), BF16, tm,
                        rope=_rope_tables(l, MM_SUB_COLS) + (2 * d,))
        mix = diff_attention_core(qkv, p['da_lam'][j].astype(F32), p['da_subln_g'][j], lambda_init, 128, 1024)
        wout = p['da_w_out'][j]
    elif m == 1:
        u = nm_matmul(x, n1, sc1, sh1, p['s5_w_in'][j].astype(BF16), F32, tm)
        ops = s5_operators(p['s5_a_re'][j], p['s5_a_im'][j], p['s5_log_dt'][j], p['s5_b_re'][j],
                           p['s5_b_im'][j], p['s5_c_re'][j], p['s5_c_im'][j])
        mix = (s5_core(u, ops), u, p['s5_d'][j])
        wout = p['s5_w_glu'][j]
        mode = "glu"
    elif m == 2:
        qkv = nm_matmul(x, n1, sc1, sh1, p['na_w_in'][j].astype(BF16), BF16, tm)
        mix = neighborhood_attention_core(qkv, p['na_rpb'][j])
        wout = p['na_w_out'][j]
    else:
        proj = nm_matmul(x, n1, sc1, sh1, _pad_cols(p['dn_w_in'][j], MM_SUB_COLS).astype(BF16), F32, tm)
        mix = gated_deltanet_core(proj, p['dn_conv_w'][j], p['dn_a_log'][j], p['dn_dt_bias'][j],
                                  p['dn_onorm_g'][j])
        wout = p['dn_w_out'][j]
    return out_ffn(x, mix, wout.astype(BF16), g1, p['norm2_g'][i], sc2, sh2, g2,
                   p['ffn_w1'][i].astype(BF16), p['ffn_w3'][i].astype(BF16), p['ffn_w2'][i].astype(BF16),
                   p['final_g'] if final else None, mode, tm_ffn, tf)


def _encoder_trunk(x, mod, p):
    depth = p['ffn_w1'].shape[0]
    for i in range(depth):
        x = _layer(x, mod[i], p, i, i == depth - 1)
    return x


def kernel(x_prompt, x_sample, c_prompt, c_sample, ada_w, ada_b, norm1_g, norm2_g, ffn_w1, ffn_w3, ffn_w2, da_w_in, da_lam, da_subln_g, da_w_out, s5_w_in, s5_a_re, s5_a_im, s5_log_dt, s5_b_re, s5_b_im, s5_c_re, s5_c_im, s5_d, s5_w_glu, na_w_in, na_rpb, na_w_out, dn_w_in, dn_conv_w, dn_a_log, dn_dt_bias, dn_onorm_g, dn_w_out, final_g):
    p = dict(norm1_g=norm1_g, norm2_g=norm2_g, ffn_w1=ffn_w1, ffn_w3=ffn_w3, ffn_w2=ffn_w2,
             da_w_in=da_w_in, da_lam=da_lam, da_subln_g=da_subln_g, da_w_out=da_w_out,
             s5_w_in=s5_w_in, s5_a_re=s5_a_re, s5_a_im=s5_a_im, s5_log_dt=s5_log_dt,
             s5_b_re=s5_b_re, s5_b_im=s5_b_im, s5_c_re=s5_c_re, s5_c_im=s5_c_im, s5_d=s5_d, s5_w_glu=s5_w_glu,
             na_w_in=na_w_in, na_rpb=na_rpb, na_w_out=na_w_out,
             dn_w_in=dn_w_in, dn_conv_w=dn_conv_w, dn_a_log=dn_a_log, dn_dt_bias=dn_dt_bias,
             dn_onorm_g=dn_onorm_g, dn_w_out=dn_w_out, final_g=final_g)
    nb = x_prompt.shape[0]
    mod = adaln(jnp.concatenate([c_prompt, c_sample], axis=0), ada_w, ada_b)
    y_prompt = _encoder_trunk(x_prompt, mod[:, :nb], p)
    y_sample = _encoder_trunk(x_sample, mod[:, nb:], p)
    return (y_prompt, y_sample)
```

```python
import functools
import math

import jax
import jax.numpy as jnp
from jax import lax
from jax.experimental import pallas as pl
from jax.experimental.pallas import tpu as pltpu

F32 = jnp.float32
BF16 = jnp.bfloat16
HI = lax.Precision.HIGHEST

EPS = 1e-6
ROPE_THETA = 10000.0
N_MIXERS = 4
GRID_W = 64
DA_HEAD_DIM = 64
Q_BLOCK = 128
S5_GROUP = 16
S5_STATE = 64
S5_CHUNK = 16
NA_HEAD_DIM = 64
NA_WIN_ROWS = 8
NA_WIN_COLS = 16
DN_HEAD_DIM = 128
DN_CONV = 4
DN_CHUNK = 64

V7X_VMEM_LIMIT = 52 * 1024 * 1024
LANES = 128
MM_SUB_ROWS = 512
MM_SUB_COLS = 256


def _cparams(sem):
    return pltpu.CompilerParams(dimension_semantics=sem, vmem_limit_bytes=V7X_VMEM_LIMIT)


def _rms(xf, g):
    return xf * lax.rsqrt(jnp.mean(xf * xf, axis=-1, keepdims=True) + EPS) * g


def _dot(a, b):
    return jnp.dot(a, b, preferred_element_type=F32)


def _dot_nt(a, b):
    return lax.dot_general(a, b, (((1,), (1,)), ((), ())), preferred_element_type=F32)


def _dot_tn(a, b):
    return lax.dot_general(a, b, (((0,), (0,)), ((), ())), preferred_element_type=F32)


def _adaln_kernel(c_ref, w_ref, b_ref, o_ref):
    c = c_ref[...]
    a = c * jax.nn.sigmoid(c)
    o_ref[0] = jnp.dot(a, w_ref[0], preferred_element_type=F32, precision=HI) + b_ref[0]


def adaln(c, ada_w, ada_b):
    depth, d, n = ada_w.shape
    rows = c.shape[0]
    tn = 1536 if n % 1536 == 0 else n
    return pl.pallas_call(
        _adaln_kernel,
        grid=(depth, n // tn),
        in_specs=[
            pl.BlockSpec((rows, d), lambda i, j: (0, 0)),
            pl.BlockSpec((1, d, tn), lambda i, j: (i, 0, j)),
            pl.BlockSpec((1, 1, tn), lambda i, j: (i, 0, j)),
        ],
        out_specs=pl.BlockSpec((1, rows, tn), lambda i, j: (i, 0, j)),
        out_shape=jax.ShapeDtypeStruct((depth, rows, n), F32),
        compiler_params=_cparams(("parallel", "parallel")),
        name="adaln",
    )(c, ada_w, ada_b.reshape(depth, 1, n))


def _swap_halves(a, half):
    n = a.shape[-1]
    lane = lax.broadcasted_iota(jnp.int32, a.shape, a.ndim - 1)
    first = (lane % (2 * half)) < half
    return jnp.where(first, pltpu.roll(a, n - half, a.ndim - 1), pltpu.roll(a, half, a.ndim - 1))


def _nm_matmul_kernel(x_ref, g_ref, sc_ref, sh_ref, w_ref, *rest, rope_cols):
    if rope_cols:
        cos_ref, sin_ref, o_ref, h_ref = rest
    else:
        o_ref, h_ref = rest
    h = _rms(x_ref[0], g_ref[...]) * (1.0 + sc_ref[0]) + sh_ref[0]
    h_ref[...] = h.astype(BF16)
    tm, n = o_ref.shape[1], o_ref.shape[2]
    sm, sn = min(tm, MM_SUB_ROWS), min(n, MM_SUB_COLS)
    for r0 in range(0, tm, sm):
        for c0 in range(0, n, sn):
            rs, cs = slice(r0, r0 + sm), slice(c0, c0 + sn)
            acc = _dot(h_ref[rs, :], w_ref[:, cs])
            if c0 < rope_cols:
                acc = acc * cos_ref[rs, :] + _swap_halves(acc, DA_HEAD_DIM // 2) * sin_ref[rs, :]
            o_ref[0, rs, cs] = acc.astype(o_ref.dtype)


def nm_matmul(x, g, sc, sh, w, out_dtype, tm, rope=None):
    b, l, d = x.shape
    n = w.shape[1]
    assert l % tm == 0 and n % min(n, MM_SUB_COLS) == 0
    in_specs = [
        pl.BlockSpec((1, tm, d), lambda bi, i: (bi, i, 0)),
        pl.BlockSpec((1, d), lambda bi, i: (0, 0)),
        pl.BlockSpec((1, 1, d), lambda bi, i: (bi, 0, 0)),
        pl.BlockSpec((1, 1, d), lambda bi, i: (bi, 0, 0)),
        pl.BlockSpec((d, n), lambda bi, i: (0, 0), pipeline_mode=pl.Buffered(1)),
    ]
    args = [x, g.reshape(1, d), sc, sh, w]
    rope_cols = 0
    if rope is not None:
        cos_t, sin_t, rope_cols = rope
        assert rope_cols % MM_SUB_COLS == 0 and cos_t.shape[1] == MM_SUB_COLS
        in_specs += [pl.BlockSpec((tm, MM_SUB_COLS), lambda bi, i: (i, 0))] * 2
        args += [cos_t, sin_t]
    return pl.pallas_call(
        functools.partial(_nm_matmul_kernel, rope_cols=rope_cols),
        grid=(b, l // tm),
        in_specs=in_specs,
        out_specs=pl.BlockSpec((1, tm, n), lambda bi, i: (bi, i, 0)),
        out_shape=jax.ShapeDtypeStruct((b, l, n), out_dtype),
        scratch_shapes=[pltpu.VMEM((tm, d), BF16)],
        compiler_params=_cparams(("parallel", "parallel")),
        name="nm_matmul",
    )(*args)


def _out_ffn_kernel(*refs, mode, final):
    it = iter(refs)
    x_ref = next(it)
    if mode == "glu":
        ys_ref, u_ref, dsk_ref = next(it), next(it), next(it)
    else:
        o_ref = next(it)
    wout_ref, g1_ref, gn_ref, sc_ref, sh_ref, g2_ref = (next(it) for _ in range(6))
    w1_ref, w3_ref, w2_ref = next(it), next(it), next(it)
    fg_ref = next(it) if final else None
    out_ref, x1_ref, h_ref, acc_ref = next(it), next(it), next(it), next(it)
    f = pl.program_id(2)
    d = x_ref.shape[-1]

    tm = x_ref.shape[1]
    sm, sn = min(tm, MM_SUB_ROWS), min(d, MM_SUB_COLS)
    row_blocks = [slice(r0, r0 + sm) for r0 in range(0, tm, sm)]

    @pl.when(f == 0)
    def _():
        for rs in row_blocks:
            if mode == "glu":
                o = jax.nn.gelu(ys_ref[0, rs, :] + dsk_ref[...] * u_ref[0, rs, :]).astype(BF16)
            else:
                o = o_ref[0, rs, :]
            for c0 in range(0, d, sn):
                cs = slice(c0, c0 + sn)
                y = _dot(o, wout_ref[:, cs])
                if mode == "glu":
                    y = y * jax.nn.sigmoid(_dot(o, wout_ref[:, slice(d + c0, d + c0 + sn)]))
                x1_ref[rs, cs] = x_ref[0, rs, cs] + g1_ref[0, :, cs] * y
            h = _rms(x1_ref[rs, :], gn_ref[...]) * (1.0 + sc_ref[0]) + sh_ref[0]
            h_ref[rs, :] = h.astype(BF16)
        acc_ref[...] = jnp.zeros_like(acc_ref)

    for rs in row_blocks:
        h = h_ref[rs, :]
        for c0 in range(0, w1_ref.shape[1], sn):
            cs = slice(c0, c0 + sn)
            a = _dot(h, w1_ref[:, cs])
            b = _dot(h, w3_ref[:, cs])
            t = (a * jax.nn.sigmoid(a) * b).astype(BF16)
            acc_ref[rs, :] += _dot(t, w2_ref[cs, :])

    @pl.when(f == pl.num_programs(2) - 1)
    def _():
        r = x1_ref[...] + g2_ref[0] * acc_ref[...]
        if final:
            r = _rms(r, fg_ref[...])
        out_ref[0] = r


def out_ffn(x, mix_in, wout, g1, gn, sc2, sh2, g2, w1, w3, w2, final_g, mode, tm, tf):
    b, l, d = x.shape
    fh = w1.shape[1]
    assert l % tm == 0 and fh % tf == 0
    wmode = pl.Buffered(1) if tf == fh else None
    tok = pl.BlockSpec((1, tm, d), lambda bi, i, f: (bi, i, 0))
    per_b = pl.BlockSpec((1, 1, d), lambda bi, i, f: (bi, 0, 0))
    vec = pl.BlockSpec((1, d), lambda bi, i, f: (0, 0))
    in_specs = [tok]
    args = [x]
    if mode == "glu":
        ys, u, dsk = mix_in
        in_specs += [tok, tok, vec]
        args += [ys, u, dsk.reshape(1, d)]
    else:
        in_specs += [tok]
        args += [mix_in]
    in_specs += [pl.BlockSpec(wout.shape, lambda bi, i, f: (0, 0), pipeline_mode=pl.Buffered(1)),
                 per_b, vec, per_b, per_b, per_b,
                 pl.BlockSpec((d, tf), lambda bi, i, f: (0, f), pipeline_mode=wmode),
                 pl.BlockSpec((d, tf), lambda bi, i, f: (0, f), pipeline_mode=wmode),
                 pl.BlockSpec((tf, d), lambda bi, i, f: (f, 0), pipeline_mode=wmode)]
    args += [wout, g1, gn.reshape(1, d), sc2, sh2, g2, w1, w3, w2]
    final = final_g is not None
    if final:
        in_specs += [vec]
        args += [final_g.reshape(1, d)]
    return pl.pallas_call(
        functools.partial(_out_ffn_kernel, mode=mode, final=final),
        grid=(b, l // tm, fh // tf),
        in_specs=in_specs,
        out_specs=tok,
        out_shape=jax.ShapeDtypeStruct((b, l, d), F32),
        scratch_shapes=[pltpu.VMEM((tm, d), F32), pltpu.VMEM((tm, d), BF16), pltpu.VMEM((tm, d), F32)],
        compiler_params=_cparams(("parallel", "parallel", "arbitrary")),
        name="out_ffn",
    )(*args)


def _block_diag_rows(q):
    lane = lax.broadcasted_iota(jnp.int32, q.shape, 1)
    zero = jnp.zeros_like(q)
    return jnp.concatenate([jnp.where(lane < 64, q, zero), jnp.where(lane >= 64, q, zero)], axis=0)


def _da_kernel(lam_ref, g_ref, q_ref, k_ref, v_ref, o_ref, *, tq, kc, lambda_init):
    l = q_ref.shape[1]
    lam = lam_ref[...]
    lam_full = (jnp.exp(jnp.sum(lam[0:1] * lam[1:2], axis=-1, keepdims=True))
                - jnp.exp(jnp.sum(lam[2:3] * lam[3:4], axis=-1, keepdims=True)) + lambda_init)
    scale = DA_HEAD_DIM ** -0.5

    def q_block(qi, carry):
        q = q_ref[0, pl.ds(pl.multiple_of(qi * tq, tq), tq), :]
        qbd = _block_diag_rows(q * scale)

        nk = l // kc
        scores = lambda ki: _dot_nt(qbd, k_ref[0, pl.ds(pl.multiple_of(ki * kc, kc), kc), :])

        def kv_step(ki, st):
            s, m, s_sum, acc = st
            s_next = scores(jnp.minimum(ki + 1, nk - 1))
            m_new = jnp.maximum(m, jnp.max(s, axis=-1, keepdims=True))
            alpha = jnp.exp(m - m_new)
            e = jnp.exp(s - m_new)
            s_sum = alpha * s_sum + jnp.sum(e, axis=-1, keepdims=True)
            acc = alpha * acc + _dot(e.astype(BF16), v_ref[0, pl.ds(pl.multiple_of(ki * kc, kc), kc), :])
            return s_next, m_new, s_sum, acc

        init = (scores(0), jnp.full((2 * tq, 1), -1e30, F32), jnp.zeros((2 * tq, 1), F32),
                jnp.zeros((2 * tq, 2 * DA_HEAD_DIM), F32))
        _, _, s_sum, acc = lax.fori_loop(0, nk, kv_step, init, unroll=True)
        o = acc / s_sum
        a = o[:tq] - lam_full * o[tq:]
        r = _rms(a, g_ref[...]) * (1.0 - lambda_init)
        o_ref[0, pl.ds(pl.multiple_of(qi * tq, tq), tq), :] = r.astype(o_ref.dtype)
        return carry

    lax.fori_loop(0, l // tq, q_block, 0)


def diff_attention_core(qkv, lam, subln_g, lambda_init, tq, kc):
    b, l, n3 = qkv.shape
    d = n3 // 3
    hd = 2 * DA_HEAD_DIM
    nh = d // hd
    tq, kc = min(tq, l), min(kc, l)
    assert l % tq == 0 and l % kc == 0
    return pl.pallas_call(
        functools.partial(_da_kernel, tq=tq, kc=kc, lambda_init=lambda_init),
        grid=(b, nh),
        in_specs=[
            pl.BlockSpec(lam.shape, lambda bi, h: (0, 0)),
            pl.BlockSpec((1, hd), lambda bi, h: (0, 0)),
            pl.BlockSpec((1, l, hd), lambda bi, h: (bi, 0, h)),
            pl.BlockSpec((1, l, hd), lambda bi, h: (bi, 0, nh + h)),
            pl.BlockSpec((1, l, hd), lambda bi, h: (bi, 0, 2 * nh + h)),
        ],
        out_specs=pl.BlockSpec((1, l, hd), lambda bi, h: (bi, 0, h)),
        out_shape=jax.ShapeDtypeStruct((b, l, d), BF16),
        compiler_params=_cparams(("parallel", "parallel")),
        name="diff_attn",
    )(lam, subln_g.reshape(1, hd), qkv, qkv, qkv)


def _na_kernel(q_ref, k_ref, v_ref, t2_ref, o_ref, *, rows, pairs, unroll):
    kr = NA_WIN_ROWS
    kw = kr * GRID_W
    scale = NA_HEAD_DIM ** -0.5
    rr = lax.broadcasted_iota(jnp.int32, (2 * GRID_W, 2 * GRID_W), 0) % GRID_W
    cc = lax.broadcasted_iota(jnp.int32, (2 * GRID_W, 2 * GRID_W), 1) % GRID_W
    cs = jnp.clip(rr - NA_WIN_COLS // 2, 0, GRID_W - NA_WIN_COLS)
    valid2 = (cc >= cs) & (cc < cs + NA_WIN_COLS)
    valid = jnp.concatenate([valid2] * (kr // 2), axis=1)
    lane = lax.broadcasted_iota(jnp.int32, (GRID_W, 2 * NA_HEAD_DIM), 1)

    def row_step(it, carry):
        jobs = []
        for u in range(unroll):
            r = it * unroll + u
            rs = jnp.clip(r - kr // 2, 0, rows - kr)
            for p in range(pairs):
                jobs.append((pl.ds(pl.multiple_of(r * GRID_W, GRID_W), GRID_W),
                             pl.ds(pl.multiple_of(rs * GRID_W, GRID_W), kw),
                             slice(p * 128, (p + 1) * 128), p, rs - r + NA_WIN_ROWS - 1))
        s = [_dot_nt(_block_diag_rows(q_ref[0, qs, cols]), k_ref[0, bs, cols]) * scale
             for qs, bs, cols, _, _ in jobs]
        bias = [jnp.concatenate([t2_ref[p, rho0 + 2 * j] for j in range(kr // 2)], axis=1)
                for _, _, _, p, rho0 in jobs]
        s = [jnp.where(valid, x + y, -1e30) for x, y in zip(s, bias)]
        m = [jnp.max(x, axis=-1, keepdims=True) for x in s]
        e = [jnp.exp(x - y) for x, y in zip(s, m)]
        den = [jnp.sum(x, axis=-1, keepdims=True) for x in e]
        o = [_dot(x.astype(BF16), v_ref[0, bs, cols]) / y for x, y, (_, bs, cols, _, _) in zip(e, den, jobs)]
        for x, (qs, _, cols, _, _) in zip(o, jobs):
            o_ref[0, qs, cols] = jnp.where(lane < NA_HEAD_DIM, x[:GRID_W], x[GRID_W:]).astype(o_ref.dtype)
        return carry

    lax.fori_loop(0, rows // unroll, row_step, 0)


def na_bias_table(rpb):
    nh = rpb.shape[0]
    cols = jnp.arange(GRID_W)
    cidx = jnp.clip(cols[None, :] - cols[:, None], -(NA_WIN_COLS - 1), NA_WIN_COLS - 1) + NA_WIN_COLS - 1
    t = rpb[:, :, cidx]
    t = t.reshape(nh // 2, 2, 2 * NA_WIN_ROWS - 1, GRID_W, GRID_W).transpose(0, 2, 1, 3, 4)
    t = t.reshape(nh // 2, 2 * NA_WIN_ROWS - 1, 2 * GRID_W, GRID_W)
    return jnp.concatenate([t[:, :-1], t[:, 1:]], axis=-1).astype(F32)


def neighborhood_attention_core(qkv, rpb):
    b, l, n3 = qkv.shape
    d = n3 // 3
    rows = l // GRID_W
    assert rows >= NA_WIN_ROWS and l % GRID_W == 0
    pairs = 2
    gw = pairs * 2 * NA_HEAD_DIM
    ng = d // gw
    t2 = na_bias_table(rpb)
    nrho = t2.shape[1]
    return pl.pallas_call(
        functools.partial(_na_kernel, rows=rows, pairs=pairs, unroll=4 if rows % 4 == 0 else 1),
        grid=(b, ng),
        in_specs=[
            pl.BlockSpec((1, l, gw), lambda bi, g: (bi, 0, g)),
            pl.BlockSpec((1, l, gw), lambda bi, g: (bi, 0, ng + g)),
            pl.BlockSpec((1, l, gw), lambda bi, g: (bi, 0, 2 * ng + g)),
            pl.BlockSpec((pairs, nrho, 2 * GRID_W, 2 * GRID_W), lambda bi, g: (g, 0, 0, 0)),
        ],
        out_specs=pl.BlockSpec((1, l, gw), lambda bi, g: (bi, 0, g)),
        out_shape=jax.ShapeDtypeStruct((b, l, d), BF16),
        compiler_params=_cparams(("parallel", "parallel")),
        name="nbr_attn",
    )(qkv, qkv, qkv, t2)


def s5_operators(a_re, a_im, log_dt, b_re, b_im, c_re, c_im):
    t = S5_CHUNK
    lam = lax.complex(a_re.astype(F32), a_im.astype(F32))
    dt = jnp.exp(log_dt.astype(F32))[..., None]
    lam_bar = jnp.exp(lam * dt)
    b_bar = ((lam_bar - 1.0) / lam)[..., None] * lax.complex(b_re.astype(F32), b_im.astype(F32))
    c_mat = lax.complex(c_re.astype(F32), c_im.astype(F32))
    taus = jnp.arange(t + 1, dtype=F32)
    pw = jnp.exp((lam * dt)[:, :, None, :] * taus[None, None, :, None])
    kern = jnp.real(jnp.einsum('dgpn,dgtn,dgnq->dgtpq', c_mat, pw[:, :, :t], b_bar, precision=HI))
    ti = jnp.arange(t)
    tau_f = ti[None, :] - ti[:, None]
    kf = jnp.where((tau_f >= 0)[None, :, :, None, None], kern[0][:, jnp.clip(tau_f, 0, t - 1)], 0.0)
    kb = jnp.where((tau_f <= 0)[None, :, :, None, None], kern[1][:, jnp.clip(-tau_f, 0, t - 1)], 0.0)
    g, p = kern.shape[1], kern.shape[3]
    wy_u = (kf + kb).transpose(0, 1, 4, 2, 3).reshape(g, t * p, t * p)

    def state_to_y(dr, powers):
        m = c_mat[dr][:, None, :, :] * pw[dr][:, powers, None, :]
        w = jnp.concatenate([jnp.real(m), -jnp.imag(m)], axis=-1)
        return w.transpose(0, 3, 1, 2).reshape(g, -1, t * p)

    wy_x = jnp.concatenate([state_to_y(0, ti + 1), state_to_y(1, t - ti)], axis=1)

    def u_to_state(dr, powers):
        m = pw[dr][:, powers, :, None] * b_bar[dr][:, None, :, :]
        w = jnp.concatenate([jnp.real(m), jnp.imag(m)], axis=2)
        return w.transpose(0, 1, 3, 2).reshape(g, t * p, -1)

    wb = jnp.concatenate([u_to_state(0, t - 1 - ti), u_to_state(1, ti)], axis=-1)
    a16 = pw[:, :, t, :]
    rot = jnp.stack([jnp.concatenate([jnp.real(a16), jnp.real(a16)], -1),
                     jnp.concatenate([-jnp.imag(a16), jnp.imag(a16)], -1)], axis=2)
    rot = rot.transpose(1, 0, 2, 3).reshape(g, 4, -1)
    return wb.astype(BF16), wy_u.astype(BF16), wy_x.astype(BF16), rot.astype(F32)


def _s5_kernel(u_ref, wb_ref, wyu_ref, wyx_ref, rot_ref, y_ref, s_ref, x_ref, *, nc, nb, gb):
    ns2 = 2 * S5_STATE
    for g in range(gb):
        s_ref[g] = _dot(u_ref[g], wb_ref[g])

    def step(c, xs):
        cb = nc - 1 - c
        rf = pl.ds(pl.multiple_of(c * nb, nb), nb)
        rb = pl.ds(pl.multiple_of(cb * nb, nb), nb)
        new = []
        for g in range(gb):
            xf, xb = xs[2 * g], xs[2 * g + 1]
            rot = rot_ref[g]
            x_ref[g, rf, 0:ns2] = xf.astype(BF16)
            x_ref[g, rb, ns2:2 * ns2] = xb.astype(BF16)
            xf = rot[0:1] * xf + rot[1:2] * pltpu.roll(xf, S5_STATE, 1) + s_ref[g, rf, 0:ns2]
            xb = rot[2:3] * xb + rot[3:4] * pltpu.roll(xb, S5_STATE, 1) + s_ref[g, rb, ns2:2 * ns2]
            new += [xf, xb]
        return tuple(new)

    lax.fori_loop(0, nc, step, tuple(jnp.zeros((nb, ns2), F32) for _ in range(2 * gb)))
    for g in range(gb):
        y_ref[g] = (_dot(u_ref[g], wyu_ref[g]) + _dot(x_ref[g], wyx_ref[g])).astype(y_ref.dtype)


def _s5_pack_kernel(u_ref, perm_ref, o_ref):
    w = o_ref.shape[2]
    for gl in range(o_ref.shape[0]):
        o_ref[gl] = _dot(u_ref[0], perm_ref[:, gl * w:(gl + 1) * w]).astype(o_ref.dtype)


def _s5_unpack_kernel(y_ref, perm_t_ref, o_ref):
    ycat = jnp.concatenate([y_ref[gl] for gl in range(y_ref.shape[0])], axis=1)
    w = y_ref.shape[2]
    for c0 in range(0, o_ref.shape[2], w):
        o_ref[0, :, c0:c0 + w] = _dot(ycat, perm_t_ref[:, c0:c0 + w]).astype(o_ref.dtype)


def _s5_relayout(x, perm, pack):
    gl = LANES // S5_GROUP
    w = S5_CHUNK * S5_GROUP
    rows = x.shape[1]
    nj = x.shape[0] if pack else x.shape[0] // gl
    tr = min(MM_SUB_ROWS, rows)
    assert rows % tr == 0
    wide = pl.BlockSpec((1, tr, gl * w), lambda j, i: (j, i, 0))
    narrow = pl.BlockSpec((gl, tr, w), lambda j, i: (j, i, 0))
    return pl.pallas_call(
        _s5_pack_kernel if pack else _s5_unpack_kernel,
        grid=(nj, rows // tr),
        in_specs=[wide if pack else narrow,
                  pl.BlockSpec(perm.shape, lambda j, i: (0, 0), pipeline_mode=pl.Buffered(1))],
        out_specs=narrow if pack else wide,
        out_shape=jax.ShapeDtypeStruct((nj * gl, rows, w) if pack else (nj, rows, gl * w), BF16),
        compiler_params=_cparams(("parallel", "parallel")),
        name="s5_pack" if pack else "s5_unpack",
    )(x, perm)


def _s5_lane_perm():
    t, gl, p = S5_CHUNK, LANES // S5_GROUP, S5_GROUP
    src = jnp.arange(t * gl * p).reshape(t, gl, p).transpose(1, 0, 2).reshape(-1)
    return (jnp.arange(t * gl * p)[:, None] == src[None, :]).astype(BF16)


def s5_core(u, ops, gb=2):
    b, l, d = u.shape
    t, p = S5_CHUNK, S5_GROUP
    g = d // p
    nc = l // t
    w = t * p
    nj = d // LANES
    wb, wy_u, wy_x, rot = ops
    perm = _s5_lane_perm()
    uj = u.astype(BF16).reshape(b, nc, t, nj, LANES).transpose(3, 1, 0, 2, 4).reshape(nj, nc * b, t * LANES)
    ur = _s5_relayout(uj, perm, pack=True)
    grp = lambda shape: pl.BlockSpec((gb,) + shape, lambda i: (i, 0, 0))
    yr = pl.pallas_call(
        functools.partial(_s5_kernel, nc=nc, nb=b, gb=gb),
        grid=(g // gb,),
        in_specs=[grp((nc * b, w)), grp(wb.shape[1:]), grp(wy_u.shape[1:]), grp(wy_x.shape[1:]), grp(rot.shape[1:])],
        out_specs=grp((nc * b, w)),
        out_shape=jax.ShapeDtypeStruct((g, nc * b, w), BF16),
        scratch_shapes=[pltpu.VMEM((gb, nc * b, 4 * S5_STATE), F32), pltpu.VMEM((gb, nc * b, 4 * S5_STATE), BF16)],
        compiler_params=_cparams(("parallel",)),
        name="s5_scan",
    )(ur, wb, wy_u, wy_x, rot)
    yj = _s5_relayout(yr, perm.T, pack=False)
    return yj.reshape(nj, nc, b, t, LANES).transpose(2, 1, 3, 0, 4).reshape(b, l, d)


def _split2(a):
    hi = a.astype(BF16)
    return hi, (a - hi.astype(F32)).astype(BF16)


def _dot3(a, b):
    (ah, al), (bh, bl) = a, b
    return _dot(ah, bh) + (_dot(ah, bl) + _dot(al, bh))


def _tri_inverse_minus_eye(ms):
    n = [-m for m in ms]
    sa = [_split2(x) for x in n]
    for _ in range(5):
        a = [_dot3(x, x) for x in sa]
        sa = [_split2(x) for x in a]
        n = [y + x + _dot3(_split2(y), sx) for x, y, sx in zip(a, n, sa)]
    return n


def _dn_kernel(hp_ref, gn_ref, wq_ref, wk_ref, wv_ref, q_ref, k_ref, v_ref, z_ref, ab_ref, o_ref,
               u_ref, wq2_ref, kd_ref, at_ref, cd_ref, of_ref, ob_ref, *, nheads, unroll):
    l = q_ref.shape[1]
    c = DN_CHUNK
    c2 = 2 * c
    n = l // c
    dk = DN_HEAD_DIM
    h = pl.program_id(1)
    lane_hp = lax.broadcasted_iota(jnp.int32, hp_ref.shape, 1)
    hp = jnp.sum(jnp.where(lane_hp == h, hp_ref[...], 0.0), axis=-1, keepdims=True)
    row = lax.broadcasted_iota(jnp.int32, (c2, c2), 0)
    col = lax.broadcasted_iota(jnp.int32, (c2, c2), 1)
    fwd = row < c
    same = fwd == (col < c)
    incl = same & ((fwd & (row >= col)) | (jnp.logical_not(fwd) & (row <= col)))
    strict = incl & (row != col)
    tri16 = incl.astype(F32).astype(BF16)
    fwd_col = fwd[:, 0:1]
    a_log = jnp.where(fwd_col, hp[0:1], hp[1:2])
    dt_b = jnp.where(fwd_col, hp[2:3], hp[3:4])
    lane_ab = lax.broadcasted_iota(jnp.int32, (c, LANES), 1)

    def conv_silu(x_ref, w_ref_, ci):
        base = pl.multiple_of(ci * c, c)
        x = x_ref[0, pl.ds(base, c), :]
        prev = x_ref[0, pl.ds(jnp.maximum(base - 8, 0), 8), :] * jnp.where(ci > 0, 1.0, 0.0)
        nxt = x_ref[0, pl.ds(jnp.minimum(base + c, l - 8), 8), :] * jnp.where(ci < n - 1, 1.0, 0.0)
        e = jnp.concatenate([prev, x, nxt], axis=0)
        ne = c + 16
        w = w_ref_[...]
        y = (w[0:1] * pltpu.roll(e, 1, 0) + w[1:2] * e
             + w[2:3] * pltpu.roll(e, ne - 1, 0) + w[3:4] * pltpu.roll(e, ne - 2, 0))[8:8 + c]
        return y * jax.nn.sigmoid(y)

    def pick(ci, j):
        ab = ab_ref[0, pl.ds(pl.multiple_of(ci * c, c), c), :]
        return jnp.sum(jnp.where(lane_ab == j * nheads + h, ab, 0.0), axis=-1, keepdims=True)

    lane = lax.broadcasted_iota(jnp.int32, (c2, dk), 1)

    per_end = unroll // 2

    def prep(it):
        cis = [it * per_end + j for j in range(per_end)] + [n - 1 - it * per_end - j for j in range(per_end)]
        two = lambda x: jnp.concatenate([x, x], axis=0)
        q = [conv_silu(q_ref, wq_ref, ci) for ci in cis]
        k = [conv_silu(k_ref, wk_ref, ci) for ci in cis]
        v = [two(conv_silu(v_ref, wv_ref, ci)) for ci in cis]
        q = [two(x * lax.rsqrt(jnp.sum(x * x, axis=-1, keepdims=True) + EPS) * (dk ** -0.5)) for x in q]
        k = [two(x * lax.rsqrt(jnp.sum(x * x, axis=-1, keepdims=True) + EPS)) for x in k]
        g_in = [jnp.concatenate([pick(ci, 0), pick(ci, 1)], axis=0) for ci in cis]
        beta = [jax.nn.sigmoid(jnp.concatenate([pick(ci, 2), pick(ci, 3)], axis=0)) for ci in cis]
        g = [-jnp.exp(a_log) * jax.nn.softplus(x + dt_b) for x in g_in]
        g_hi = [x.astype(BF16).astype(F32) for x in g]
        r1 = [x - y for x, y in zip(g, g_hi)]
        g_mid = [x.astype(BF16).astype(F32) for x in r1]
        g3 = [jnp.where(lane == 0, a_, jnp.where(lane == 1, b_, jnp.where(lane == 2, r_ - b_, 0.0)))
              for a_, b_, r_ in zip(g_hi, g_mid, r1)]
        gc = [jnp.broadcast_to(jnp.sum(_dot(tri16, x.astype(BF16)), axis=-1, keepdims=True), (c2, dk)) for x in g3]
        decay = [jnp.where(incl, jnp.exp(jnp.where(incl, x - x.T, 0.0)), 0.0) for x in gc]
        k16 = [x.astype(BF16) for x in k]
        kbeta = [x * y for x, y in zip(k, beta)]
        m = [jnp.where(strict, _dot_nt(x.astype(BF16), y) * z, 0.0) for x, y, z in zip(kbeta, k16, decay)]
        egc = [jnp.exp(x) for x in gc]
        rhs = [jnp.concatenate([x * y, z * w_], axis=1) for x, y, z, w_ in zip(v, beta, kbeta, egc)]
        ninv = _tri_inverse_minus_eye(m)
        sol = [x + _dot(y.astype(BF16), x.astype(BF16)) for x, y in zip(rhs, ninv)]
        attn = [(_dot_nt(x.astype(BF16), y) * z).astype(BF16) for x, y, z in zip(q, k16, decay)]
        for j, ci in enumerate(cis):
            gcj = gc[j]
            gc_last = jnp.where(fwd, gcj[c - 1:c], gcj[c:c + 1])
            qd = (q[j] * egc[j]).astype(BF16)
            w16 = sol[j][:, dk:].astype(BF16)
            kd = (k[j] * jnp.exp(gc_last - gcj)).astype(BF16)
            r1_ = pl.ds(pl.multiple_of(ci * c, c), c)
            r2_ = pl.ds(pl.multiple_of(ci * c2, c2), c2)
            r8_ = pl.ds(pl.multiple_of(ci * 8, 8), 8)
            for dr, half in enumerate((slice(0, c), slice(c, c2))):
                u_ref[dr, r1_, :] = sol[j][half, :dk]
                wq2_ref[dr, r2_, :] = jnp.concatenate([w16[half], qd[half]], axis=0)
                kd_ref[dr, r1_, :] = kd[half]
                at_ref[dr, r1_, :] = attn[j][half]
            cd_ref[0, r8_, :] = jnp.broadcast_to(jnp.exp(gcj[c - 1:c]), (8, dk))
            cd_ref[1, r8_, :] = jnp.broadcast_to(jnp.exp(gcj[c:c + 1]), (8, dk))

    def step(i, st):
        sf, sb = st
        ib = n - 1 - i
        rf = pl.ds(pl.multiple_of(i * c, c), c)
        rb = pl.ds(pl.multiple_of(ib * c, c), c)
        sf16, sb16 = sf.astype(BF16), sb.astype(BF16)
        pf = _dot(wq2_ref[0, pl.ds(pl.multiple_of(i * c2, c2), c2), :], sf16)
        pb = _dot(wq2_ref[1, pl.ds(pl.multiple_of(ib * c2, c2), c2), :], sb16)
        v_new = jnp.concatenate([u_ref[0, rf, :] - pf[:c], u_ref[1, rb, :] - pb[:c]], axis=0)
        v16 = v_new.astype(BF16)
        attn = jnp.concatenate([at_ref[0, rf, :], at_ref[1, rb, :]], axis=0)
        o = _dot(attn, v16) + jnp.concatenate([pf[c:], pb[c:]], axis=0)
        of_ref[rf, :] = o[:c]
        ob_ref[rb, :] = o[c:]
        cdf = cd_ref[0, pl.ds(pl.multiple_of(i * 8, 8), 1), :]
        cdb = cd_ref[1, pl.ds(pl.multiple_of(ib * 8, 8), 1), :]
        sf = sf * cdf + _dot_tn(kd_ref[0, rf, :], v16[:c])
        sb = sb * cdb + _dot_tn(kd_ref[1, rb, :], v16[c:])
        return sf, sb

    def prep_and_steps(it, st):
        for j in range(per_end):
            st = step((it - 1) * per_end + j, st)
        prep(it)
        return st

    zero = jnp.zeros((dk, dk), F32)
    n_prep = n // unroll
    prep(jnp.int32(0))
    st = lax.fori_loop(1, n_prep, prep_and_steps, (zero, zero))
    lax.fori_loop((n_prep - 1) * per_end, n, step, st, unroll=2)

    def finish(ci, carry):
        rows = pl.ds(pl.multiple_of(ci * c, c), c)
        z = z_ref[0, rows, :]
        r = _rms(of_ref[rows, :] + ob_ref[rows, :], gn_ref[...]) * (z * jax.nn.sigmoid(z))
        o_ref[0, rows, :] = r.astype(o_ref.dtype)
        return carry

    lax.fori_loop(0, n, finish, 0, unroll=4)


def gated_deltanet_core(proj, conv_w, a_log, dt_bias, onorm_g):
    b, l, _ = proj.shape
    d = conv_w.shape[1] // 3
    dk = DN_HEAD_DIM
    nh = d // dk
    n = l // DN_CHUNK
    assert l % DN_CHUNK == 0 and dk == LANES == 2 * DN_CHUNK
    unroll = 8
    assert n % unroll == 0
    hp =jnp.concatenate([a_log, dt_bias], axis=0).astype(F32)
    tok = lambda off: pl.BlockSpec((1, l, dk), lambda bi, h: (bi, 0, off + h))
    cw = lambda off: pl.BlockSpec((DN_CONV, dk), lambda bi, h: (0, off + h))
    return pl.pallas_call(
        functools.partial(_dn_kernel, nheads=nh, unroll=unroll),
        grid=(b, nh),
        in_specs=[
            pl.BlockSpec(hp.shape, lambda bi, h: (0, 0)),
            pl.BlockSpec((1, dk), lambda bi, h: (0, 0)),
            cw(0), cw(nh), cw(2 * nh),
            tok(0), tok(nh), tok(2 * nh), tok(3 * nh),
            pl.BlockSpec((1, l, dk), lambda bi, h: (bi, 0, 4 * nh)),
        ],
        out_specs=pl.BlockSpec((1, l, dk), lambda bi, h: (bi, 0, h)),
        out_shape=jax.ShapeDtypeStruct((b, l, d), BF16),
        scratch_shapes=[
            pltpu.VMEM((2, l, dk), F32), pltpu.VMEM((2, 2 * l, dk), BF16), pltpu.VMEM((2, l, dk), BF16),
            pltpu.VMEM((2, l, 2 * DN_CHUNK), BF16), pltpu.VMEM((2, n * 8, dk), F32),
            pltpu.VMEM((l, dk), F32), pltpu.VMEM((l, dk), F32),
        ],
        compiler_params=_cparams(("parallel", "parallel")),
        name="gated_deltanet",
    )(hp, onorm_g.reshape(1, dk), conv_w, conv_w, conv_w, proj, proj, proj, proj, proj)


def _rope_tables(l, width):
    half = DA_HEAD_DIM // 2
    inv = jnp.power(ROPE_THETA, -jnp.arange(half, dtype=F32) * 2.0 / DA_HEAD_DIM)
    ang = jnp.arange(l, dtype=F32)[:, None] * inv[None, :]
    cos, sin = jnp.cos(ang), jnp.sin(ang)
    reps = width // DA_HEAD_DIM
    return (jnp.tile(jnp.concatenate([cos, cos], axis=-1), (1, reps)),
            jnp.tile(jnp.concatenate([-sin, sin], axis=-1), (1, reps)))


def _pad_cols(w, mult):
    pad = -w.shape[1] % mult
    return jnp.pad(w, ((0, 0), (0, pad))) if pad else w


def _layer(x, mod_i, p, i, final):
    b, l, d = x.shape
    tm = min(512, l)
    tm_ffn = min(512, l)
    tf = p['ffn_w1'].shape[2]
    m, j = i % N_MIXERS, i // N_MIXERS
    sh1, sc1, g1, sh2, sc2, g2 = [mod_i[:, None, k * d:(k + 1) * d] for k in range(6)]
    n1 = p['norm1_g'][i]
    mode = "linear"
    if m == 0:
        lambda_init = 0.8 - 0.6 * math.exp(-0.3 * i)
        qkv = nm_matmul(x, n1, sc1, sh1, p['da_w_in'][j].astype(BF16), BF16, tm,
                        rope=_rope_tables(l, MM_SUB_COLS) + (2 * d,))
        mix = diff_attention_core(qkv, p['da_lam'][j].astype(F32), p['da_subln_g'][j], lambda_init, 256, 1024)
        wout = p['da_w_out'][j]
    elif m == 1:
        u = nm_matmul(x, n1, sc1, sh1, p['s5_w_in'][j].astype(BF16), F32, tm)
        ops = s5_operators(p['s5_a_re'][j], p['s5_a_im'][j], p['s5_log_dt'][j], p['s5_b_re'][j],
                           p['s5_b_im'][j], p['s5_c_re'][j], p['s5_c_im'][j])
        mix = (s5_core(u, ops), u, p['s5_d'][j])
        wout = p['s5_w_glu'][j]
        mode = "glu"
    elif m == 2:
        qkv = nm_matmul(x, n1, sc1, sh1, p['na_w_in'][j].astype(BF16), BF16, tm)
        mix = neighborhood_attention_core(qkv, p['na_rpb'][j])
        wout = p['na_w_out'][j]
    else:
        proj = nm_matmul(x, n1, sc1, sh1, _pad_cols(p['dn_w_in'][j], MM_SUB_COLS).astype(BF16), F32, tm)
        mix = gated_deltanet_core(proj, p['dn_conv_w'][j], p['dn_a_log'][j], p['dn_dt_bias'][j],
                                  p['dn_onorm_g'][j])
        wout = p['dn_w_out'][j]
    return out_ffn(x, mix, wout.astype(BF16), g1, p['norm2_g'][i], sc2, sh2, g2,
                   p['ffn_w1'][i].astype(BF16), p['ffn_w3'][i].astype(BF16), p['ffn_w2'][i].astype(BF16),
                   p['final_g'] if final else None, mode, tm_ffn, tf)


def _encoder_trunk(x, mod, p):
    depth = p['ffn_w1'].shape[0]
    for i in range(depth):
        x = _layer(x, mod[i], p, i, i == depth - 1)
    return x


def kernel(x_prompt, x_sample, c_prompt, c_sample, ada_w, ada_b, norm1_g, norm2_g, ffn_w1, ffn_w3, ffn_w2, da_w_in, da_lam, da_subln_g, da_w_out, s5_w_in, s5_a_re, s5_a_im, s5_log_dt, s5_b_re, s5_b_im, s5_c_re, s5_c_im, s5_d, s5_w_glu, na_w_in, na_rpb, na_w_out, dn_w_in, dn_conv_w, dn_a_log, dn_dt_bias, dn_onorm_g, dn_w_out, final_g):
    p = dict(norm1_g=norm1_g, norm2_g=norm2_g, ffn_w1=ffn_w1, ffn_w3=ffn_w3, ffn_w2=ffn_w2,
             da_w_in=da_w_in, da_lam=da_lam, da_subln_g=da_subln_g, da_w_out=da_w_out,
             s5_w_in=s5_w_in, s5_a_re=s5_a_re, s5_a_im=s5_a_im, s5_log_dt=s5_log_dt,
             s5_b_re=s5_b_re, s5_b_im=s5_b_im, s5_c_re=s5_c_re, s5_c_im=s5_c_im, s5_d=s5_d, s5_w_glu=s5_w_glu,
             na_w_in=na_w_in, na_rpb=na_rpb, na_w_out=na_w_out,
             dn_w_in=dn_w_in, dn_conv_w=dn_conv_w, dn_a_log=dn_a_log, dn_dt_bias=dn_dt_bias,
             dn_onorm_g=dn_onorm_g, dn_w_out=dn_w_out, final_g=final_g)
    nb = x_prompt.shape[0]
    mod = adaln(jnp.concatenate([c_prompt, c_sample], axis=0), ada_w, ada_b)
    y_prompt = _encoder_trunk(x_prompt, mod[:, :nb], p)
    y_sample = _encoder_trunk(x_sample, mod[:, nb:], p)
    return (y_prompt, y_sample)
```
